```python
import math
import jax, jax.numpy as jnp
from jax import lax
import numpy as np

D_MODEL = 1024
BATCH = 4
SEQ = 4096
DEPTH = 2

N_A_LAYERS = DEPTH // 2
N_B_LAYERS = DEPTH - N_A_LAYERS
CONV_WIDTH = 31
FOX_HEADS = 16
FOX_HEAD_DIM = D_MODEL // FOX_HEADS
MEM_LEN = 256
MEM_HEADS = 4
MEM_HEAD_DIM = D_MODEL // MEM_HEADS
D_FF = 4 * D_MODEL
Q_BLOCK = 128
RMS_EPS = 1e-6
LN_EPS = 1e-5

kernel_name = "yoco_conformer_fox_hybrid"


def rmsnorm(x, g):
    xf = x.astype(jnp.float32)
    y = xf * lax.rsqrt(jnp.mean(xf * xf, axis=-1, keepdims=True) + RMS_EPS)
    return (y * g.astype(jnp.float32)).astype(x.dtype)


def layernorm(x, g, b):
    xf = x.astype(jnp.float32)
    mu = jnp.mean(xf, axis=-1, keepdims=True)
    var = jnp.mean(jnp.square(xf - mu), axis=-1, keepdims=True)
    y = (xf - mu) * lax.rsqrt(var + LN_EPS)
    return (y * g.astype(jnp.float32) + b.astype(jnp.float32)).astype(x.dtype)


def conformer_conv(h, w_pw1, b_pw1, w_dw, b_dw, ln_g, ln_b, w_pw2, b_pw2):
    u = h @ w_pw1 + b_pw1
    a, gate = jnp.split(u, 2, axis=-1)
    u = a * jax.nn.sigmoid(gate)
    u = lax.conv_general_dilated(
        u, w_dw[:, None, :].astype(u.dtype),
        window_strides=(1,), padding=[(CONV_WIDTH - 1, 0)],
        dimension_numbers=("NWC", "WIO", "NWC"),
        feature_group_count=D_MODEL) + b_dw
    u = jax.nn.silu(layernorm(u, ln_g, ln_b))
    return u @ w_pw2 + b_pw2


def fox_attention(h, w_q, w_o, k, v, c_bhs):
    B, S, _ = h.shape
    nb = S // Q_BLOCK
    q = (h @ w_q).reshape(B, nb, Q_BLOCK, FOX_HEADS, FOX_HEAD_DIM).transpose(1, 0, 2, 3, 4)
    c_q = c_bhs.reshape(B, FOX_HEADS, nb, Q_BLOCK).transpose(2, 0, 1, 3)
    key_pos = jnp.arange(S)
    scale = 1.0 / math.sqrt(FOX_HEAD_DIM)

    def block(args):
        qi, ci, i = args
        s = jnp.einsum("bqhd,bkhd->bhqk", qi, k).astype(jnp.float32) * scale
        s = s + (ci[..., :, None] - c_bhs[:, :, None, :])
        q_pos = i * Q_BLOCK + jnp.arange(Q_BLOCK)
        mask = key_pos[None, :] <= q_pos[:, None]
        s = jnp.where(mask[None, None], s, -jnp.inf)
        p = jax.nn.softmax(s, axis=-1)
        return jnp.einsum("bhqk,bkhd->bqhd", p.astype(v.dtype), v)

    out = lax.map(block, (q, c_q, jnp.arange(nb)))
    out = out.transpose(1, 0, 2, 3, 4).reshape(B, S, D_MODEL)
    return out @ w_o


def memory_cross_attention(h, mem_n, w_q, w_k, w_v, w_o):
    B, S, _ = h.shape
    q = (h @ w_q).reshape(B, S, MEM_HEADS, MEM_HEAD_DIM)
    k = (mem_n @ w_k).reshape(B, MEM_LEN, MEM_HEADS, MEM_HEAD_DIM)
    v = (mem_n @ w_v).reshape(B, MEM_LEN, MEM_HEADS, MEM_HEAD_DIM)
    s = jnp.einsum("bshd,bmhd->bhsm", q, k).astype(jnp.float32) / math.sqrt(MEM_HEAD_DIM)
    p = jax.nn.softmax(s, axis=-1)
    o = jnp.einsum("bhsm,bmhd->bshd", p.astype(v.dtype), v).reshape(B, S, D_MODEL)
    return o @ w_o


def sqrelu_mlp(h, w1, w2):
    return jnp.square(jax.nn.relu(h @ w1)) @ w2


def setup_inputs(seed: int = 0) -> dict:
    key = jax.random.key(seed)
    ks = iter(jax.random.split(key, 40))

    def nrm(shape, scale):
        return jax.random.normal(next(ks), shape, jnp.float32) * scale

    def gain(shape):
        return 1.0 + nrm(shape, 0.05)

    D = D_MODEL
    return {
        "x": nrm((BATCH, SEQ, D), 1.0),
        "mem": nrm((BATCH, MEM_LEN, D), 1.0),
        "norm_mix_g": gain((DEPTH, D)),
        "norm_mem_g": gain((DEPTH, D)),
        "norm_memsrc_g": gain((DEPTH, D)),
        "norm_ff_g": gain((DEPTH, D)),
        "mem_wq": nrm((DEPTH, D, D), D ** -0.5),
        "mem_wk": nrm((DEPTH, D, D), D ** -0.5),
        "mem_wv": nrm((DEPTH, D, D), D ** -0.5),
        "mem_wo": nrm((DEPTH, D, D), 0.5 * D ** -0.5),
        "ff_w1": nrm((DEPTH, D, D_FF), D ** -0.5),
        "ff_w2": nrm((DEPTH, D_FF, D), 0.5 * D_FF ** -0.5),
        "conv_pw1_w": nrm((N_A_LAYERS, D, 2 * D), D ** -0.5),
        "conv_pw1_b": nrm((N_A_LAYERS, 2 * D), 0.02),
        "conv_dw_w": nrm((N_A_LAYERS, CONV_WIDTH, D), CONV_WIDTH ** -0.5),
        "conv_dw_b": nrm((N_A_LAYERS, D), 0.02),
        "conv_ln_g": gain((N_A_LAYERS, D)),
        "conv_ln_b": nrm((N_A_LAYERS, D), 0.02),
        "conv_pw2_w": nrm((N_A_LAYERS, D, D), 0.5 * D ** -0.5),
        "conv_pw2_b": nrm((N_A_LAYERS, D), 0.02),
        "kv_norm_g": gain((D,)),
        "kvf_w": nrm((D, 2 * D + FOX_HEADS), D ** -0.5),
        "fgate_b": 3.0 + nrm((FOX_HEADS,), 0.5),
        "fox_wq": nrm((N_B_LAYERS, D, D), D ** -0.5),
        "fox_wo": nrm((N_B_LAYERS, D, D), 0.5 * D ** -0.5),
        "final_norm_g": gain((D,)),
    }


def reference(x, mem, norm_mix_g, norm_mem_g, norm_memsrc_g, norm_ff_g,
              mem_wq, mem_wk, mem_wv, mem_wo, ff_w1, ff_w2,
              conv_pw1_w, conv_pw1_b, conv_dw_w, conv_dw_b, conv_ln_g, conv_ln_b,
              conv_pw2_w, conv_pw2_b,
              kv_norm_g, kvf_w, fgate_b, fox_wq, fox_wo, final_norm_g):
    B, S, D = x.shape
    k = v = c_bhs = None
    for l in range(DEPTH):
        if l == N_A_LAYERS:
            hk = rmsnorm(x, kv_norm_g)
            kvf = hk @ kvf_w
            k = kvf[..., :D].reshape(B, S, FOX_HEADS, FOX_HEAD_DIM)
            v = kvf[..., D:2 * D].reshape(B, S, FOX_HEADS, FOX_HEAD_DIM)
            log_f = jax.nn.log_sigmoid((kvf[..., 2 * D:] + fgate_b).astype(jnp.float32))
            c_bhs = jnp.cumsum(log_f, axis=1).transpose(0, 2, 1)
        h = rmsnorm(x, norm_mix_g[l])
        if l < N_A_LAYERS:
            a = l
            x = x + conformer_conv(h, conv_pw1_w[a], conv_pw1_b[a], conv_dw_w[a], conv_dw_b[a],
                                   conv_ln_g[a], conv_ln_b[a], conv_pw2_w[a], conv_pw2_b[a])
        else:
            bl = l - N_A_LAYERS
            x = x + fox_attention(h, fox_wq[bl], fox_wo[bl], k, v, c_bhs)
        h = rmsnorm(x, norm_mem_g[l])
        mem_n = rmsnorm(mem, norm_memsrc_g[l])
        x = x + memory_cross_attention(h, mem_n, mem_wq[l], mem_wk[l], mem_wv[l], mem_wo[l])
        h = rmsnorm(x, norm_ff_g[l])
        x = x + sqrelu_mlp(h, ff_w1[l], ff_w2[l])
    return rmsnorm(x, final_norm_g)
```

```python
import functools
import math

import jax
import jax.numpy as jnp
from jax import lax
from jax.experimental import pallas as pl
from jax.experimental.pallas import tpu as pltpu

F32 = jnp.float32
BF16 = jnp.bfloat16

CONV_WIDTH = 31
FOX_HEADS = 16
MEM_HEADS = 4
RMS_EPS = 1e-6
LN_EPS = 1e-5
LOG2E = 1.4426950408889634

LANES = 128
SUBLANES = 8
HALO = 32
TM = 512
TQ = 512
VMEM_LIMIT = 56 * 1024 * 1024


def _rms(x, g):
    return x * lax.rsqrt(jnp.mean(x * x, axis=-1, keepdims=True) + RMS_EPS) * g


def _mm(a, w):
    return jnp.dot(a.astype(BF16), w, preferred_element_type=F32)


def _mm_nt(a, b):
    return lax.dot_general(a, b, (((1,), (1,)), ((), ())), preferred_element_type=F32)


def _resident(shape, index_map):
    return pl.BlockSpec(shape, index_map, pipeline_mode=pl.Buffered(1))


def _mem_kv_kernel(mem_ref, g_ref, wk_ref, wv_ref, k_ref, v_ref):
    mn = _rms(mem_ref[0], g_ref[0]).astype(BF16)
    k_ref[0, 0] = jnp.dot(mn, wk_ref[0], preferred_element_type=F32).astype(BF16)
    v_ref[0, 0] = jnp.dot(mn, wv_ref[0], preferred_element_type=F32).astype(BF16)


def _mem_kv(mem, g, wk, wv):
    B, M, D = mem.shape
    L = wk.shape[0]
    out = jax.ShapeDtypeStruct((L, B, M, D), BF16)
    return pl.pallas_call(
        _mem_kv_kernel,
        grid=(L, B),
        in_specs=[
            pl.BlockSpec((1, M, D), lambda l, b: (b, 0, 0)),
            pl.BlockSpec((1, 1, D), lambda l, b: (l, 0, 0)),
            pl.BlockSpec((1, D, D), lambda l, b: (l, 0, 0)),
            pl.BlockSpec((1, D, D), lambda l, b: (l, 0, 0)),
        ],
        out_specs=[
            pl.BlockSpec((1, 1, M, D), lambda l, b: (l, b, 0, 0)),
            pl.BlockSpec((1, 1, M, D), lambda l, b: (l, b, 0, 0)),
        ],
        out_shape=[out, out],
        compiler_params=pltpu.CompilerParams(
            dimension_semantics=("arbitrary", "arbitrary"), vmem_limit_bytes=VMEM_LIMIT),
        name="mem_kv",
    )(mem, g.reshape(L, 1, D), wk, wv)


CONV_ROWS = 64
CONV_COLS = LANES


def _conv_mix_kernel(x_ref, g_ref, w1_ref, b1_ref, wdw_ref, bdw_ref, lng_ref, lnb_ref, w2_ref, b2_ref,
                     o_ref, ubuf, cbuf):
    tm, d = x_ref.shape[1], x_ref.shape[2]

    @pl.when(pl.program_id(1) == 0)
    def _():
        ubuf[0:HALO, :] = jnp.zeros((HALO, d), F32)

    x = x_ref[0]
    h = _rms(x, g_ref[...])
    u = _mm(h, w1_ref[...]) + b1_ref[...]
    ubuf[HALO:HALO + tm, :] = u[:, :d] * jax.nn.sigmoid(u[:, d:])

    base = HALO - (CONV_WIDTH - 1)
    n_col = d // CONV_COLS

    def chunk(i, carry):
        r0 = pl.multiple_of((i // n_col) * CONV_ROWS, CONV_ROWS)
        c0 = pl.multiple_of((i % n_col) * CONV_COLS, CONV_COLS)
        win = ubuf[pl.ds(r0, CONV_ROWS + HALO), pl.ds(c0, CONV_COLS)]
        acc = jnp.broadcast_to(bdw_ref[:, pl.ds(c0, CONV_COLS)], (CONV_ROWS, CONV_COLS))
        for r in range(SUBLANES):
            taps = [o for o in range(base, base + CONV_WIDTH) if o % SUBLANES == r]
            hi = max(taps) - r + CONV_ROWS
            sh = win[r:r + hi]
            for o in taps:
                acc = acc + wdw_ref[o - base:o - base + 1, pl.ds(c0, CONV_COLS)] * sh[o - r:o - r + CONV_ROWS]
        cbuf[pl.ds(r0, CONV_ROWS), pl.ds(c0, CONV_COLS)] = acc
        return carry

    lax.fori_loop(0, (tm // CONV_ROWS) * n_col, chunk, 0)
    ubuf[0:HALO, :] = ubuf[tm:tm + HALO, :]

    c = cbuf[...]
    mu = jnp.mean(c, axis=-1, keepdims=True)
    cc = c - mu
    var = jnp.mean(cc * cc, axis=-1, keepdims=True)
    y = cc * lax.rsqrt(var + LN_EPS) * lng_ref[...] + lnb_ref[...]
    y = y * jax.nn.sigmoid(y)
    o_ref[0] = x + _mm(y, w2_ref[...]) + b2_ref[...]


def _conv_mix(x, g, w1, b1, wdw, bdw, lng, lnb, w2, b2):
    B, S, D = x.shape
    row = lambda v: v.reshape(1, -1)
    const = lambda b, s: (0, 0)
    return pl.pallas_call(
        _conv_mix_kernel,
        grid=(B, S // TM),
        in_specs=[
            pl.BlockSpec((1, TM, D), lambda b, s: (b, s, 0)),
            _resident((1, D), const),
            _resident((D, 2 * D), const),
            _resident((1, 2 * D), const),
            _resident((CONV_WIDTH, D), const),
            _resident((1, D), const),
            _resident((1, D), const),
            _resident((1, D), const),
            _resident((D, D), const),
            _resident((1, D), const),
        ],
        out_specs=pl.BlockSpec((1, TM, D), lambda b, s: (b, s, 0)),
        out_shape=jax.ShapeDtypeStruct((B, S, D), F32),
        scratch_shapes=[pltpu.VMEM((TM + HALO, D), F32), pltpu.VMEM((TM, D), F32)],
        compiler_params=pltpu.CompilerParams(
            dimension_semantics=("arbitrary", "arbitrary"), vmem_limit_bytes=VMEM_LIMIT),
        name="conv_mix",
    )(x, row(g), w1, row(b1), wdw, row(bdw), row(lng), row(lnb), w2, row(b2))


FF_CHUNK = 1024


def _tail_kernel(*refs, has_attn, final):
    it = iter(refs)
    x_ref = next(it)
    if has_attn:
        oin_ref, fwo_ref = next(it), next(it)
    gmem_ref, wq_ref, k_ref, v_ref, wo_ref, gff_ref, w1_ref, w2_ref = (next(it) for _ in range(8))
    if final:
        gfin_ref = next(it)
    out_ref = next(it)

    d = x_ref.shape[2]
    dh = d // MEM_HEADS
    x = x_ref[0]
    if has_attn:
        x = x + jnp.dot(oin_ref[0], fwo_ref[0], preferred_element_type=F32)

    q = (_mm(_rms(x, gmem_ref[0]), wq_ref[0]) * (1.0 / math.sqrt(dh))).astype(BF16)
    heads = []
    for hd in range(MEM_HEADS):
        sl = slice(hd * dh, (hd + 1) * dh)
        s = _mm_nt(q[:, sl], k_ref[0, 0, :, sl])
        p = jnp.exp(s - jnp.max(s, axis=-1, keepdims=True))
        l = jnp.sum(p, axis=-1, keepdims=True)
        heads.append(_mm(p, v_ref[0, 0, :, sl]) / l)
    x = x + _mm(jnp.concatenate(heads, axis=-1), wo_ref[0])

    hb = _rms(x, gff_ref[0]).astype(BF16)
    acc = jnp.zeros_like(x)
    for c in range(w1_ref.shape[2] // FF_CHUNK):
        sl = slice(c * FF_CHUNK, (c + 1) * FF_CHUNK)
        a = jnp.maximum(jnp.dot(hb, w1_ref[0, :, sl], preferred_element_type=F32), 0.0)
        acc = acc + _mm(a * a, w2_ref[0, sl, :])
    x = x + acc
    if final:
        x = _rms(x, gfin_ref[...])
    out_ref[0] = x


def _tail(x, layer, gmem, wq, k, v, wo, gff, w1, w2, attn=None, gfin=None):
    B, S, D = x.shape
    L = wq.shape[0]
    M = k.shape[2]
    F = w1.shape[2]
    lay = lambda b, s: (layer, 0, 0)
    tile = pl.BlockSpec((1, TM, D), lambda b, s: (b, s, 0))
    args, specs = [x], [tile]
    if attn is not None:
        o_in, fwo, fl = attn
        args += [o_in, fwo]
        specs += [pl.BlockSpec((1, TM, D), lambda b, s: (b, s, 0)),
                  _resident((1, D, D), lambda b, s: (fl, 0, 0))]
    args += [gmem.reshape(L, 1, D), wq, k, v, wo, gff.reshape(L, 1, D), w1, w2]
    specs += [
        _resident((1, 1, D), lay),
        _resident((1, D, D), lay),
        pl.BlockSpec((1, 1, M, D), lambda b, s: (layer, b, 0, 0)),
        pl.BlockSpec((1, 1, M, D), lambda b, s: (layer, b, 0, 0)),
        _resident((1, D, D), lay),
        _resident((1, 1, D), lay),
        _resident((1, D, F), lay),
        _resident((1, F, D), lay),
    ]
    if gfin is not None:
        args.append(gfin.reshape(1, D))
        specs.append(_resident((1, D), lambda b, s: (0, 0)))
    return pl.pallas_call(
        functools.partial(_tail_kernel, has_attn=attn is not None, final=gfin is not None),
        grid=(B, S // TM),
        in_specs=specs,
        out_specs=tile,
        out_shape=jax.ShapeDtypeStruct((B, S, D), F32),
        compiler_params=pltpu.CompilerParams(
            dimension_semantics=("arbitrary", "arbitrary"), vmem_limit_bytes=VMEM_LIMIT),
        name=f"tail{layer}",
    )(*args)


def _kvfq_kernel(x_ref, gkv_ref, gq_ref, wk_ref, wvt_ref, wf_ref, fb_ref, wq_ref,
                 k_ref, vt_ref, q_ref, ccol_ref, crow_ref, carry):
    tm = x_ref.shape[1]
    dh = x_ref.shape[2] // FOX_HEADS

    @pl.when(pl.program_id(1) == 0)
    def _():
        carry[...] = jnp.zeros_like(carry)

    x = x_ref[0]
    hk = _rms(x, gkv_ref[...]).astype(BF16)
    k_ref[0] = jnp.dot(hk, wk_ref[...], preferred_element_type=F32).astype(BF16)
    vt_ref[0] = _mm_nt(wvt_ref[...], hk).astype(BF16)

    f = jnp.dot(hk, wf_ref[...], preferred_element_type=F32) + fb_ref[...]
    log_f = jnp.minimum(f, 0.0) - jnp.log(1.0 + jnp.exp(-jnp.abs(f)))
    tri = (lax.broadcasted_iota(jnp.int32, (tm, tm), 1)
           <= lax.broadcasted_iota(jnp.int32, (tm, tm), 0)).astype(F32)
    c = jnp.dot(tri, log_f, preferred_element_type=F32, precision=lax.Precision.HIGHEST) + carry[0:1, :]
    carry[0:1, :] = c[tm - 1:tm, :]
    c2 = c * LOG2E
    ccol_ref[0] = c2
    crow_ref[0] = c2.T[0:FOX_HEADS, :]

    hq = _rms(x, gq_ref[...])
    q_ref[0] = (_mm(hq, wq_ref[0]) * (LOG2E / math.sqrt(dh))).astype(BF16)


def _kvfq(x, gkv, gq, wk, wvt, wf, fb, wq, layer):
    B, S, D = x.shape
    const = lambda b, s: (0, 0)
    tile = pl.BlockSpec((1, TM, D), lambda b, s: (b, s, 0))
    return pl.pallas_call(
        _kvfq_kernel,
        grid=(B, S // TM),
        in_specs=[
            tile,
            _resident((1, D), const),
            _resident((1, D), const),
            _resident((D, D), const),
            _resident((D, D), const),
            _resident((D, LANES), const),
            _resident((1, LANES), const),
            _resident((1, D, D), lambda b, s: (layer, 0, 0)),
        ],
        out_specs=[
            tile,
            pl.BlockSpec((1, D, TM), lambda b, s: (b, 0, s)),
            tile,
            pl.BlockSpec((1, TM, LANES), lambda b, s: (b, s, 0)),
            pl.BlockSpec((1, FOX_HEADS, TM), lambda b, s: (b, 0, s)),
        ],
        out_shape=[
            jax.ShapeDtypeStruct((B, S, D), BF16),
            jax.ShapeDtypeStruct((B, D, S), BF16),
            jax.ShapeDtypeStruct((B, S, D), BF16),
            jax.ShapeDtypeStruct((B, S, LANES), F32),
            jax.ShapeDtypeStruct((B, FOX_HEADS, S), F32),
        ],
        scratch_shapes=[pltpu.VMEM((8, LANES), F32)],
        compiler_params=pltpu.CompilerParams(
            dimension_semantics=("arbitrary", "arbitrary"), vmem_limit_bytes=VMEM_LIMIT),
        name="kvfq",
    )(x, gkv.reshape(1, D), gq.reshape(1, D), wk, wvt, wf, fb, wq)


def _fox_kernel(q_ref, k_ref, vt_ref, ccol_ref, crow_ref, o_ref, cb):
    tq = q_ref.shape[1]
    dh = LANES // 2
    pair = pl.program_id(1)
    qi = pl.program_id(2)
    n_rep = tq // LANES

    @pl.when(qi == 0)
    def _():
        lane = lax.broadcasted_iota(jnp.int32, ccol_ref.shape[1:], 1)
        for hh in range(2):
            col = jnp.sum(jnp.where(lane == 2 * pair + hh, ccol_ref[0], 0.0), axis=1, keepdims=True)
            cb[hh] = jnp.broadcast_to(col, cb.shape[1:])

    q = q_ref[0]
    q_lane = lax.broadcasted_iota(jnp.int32, q.shape, 1)
    causal = (lax.broadcasted_iota(jnp.int32, (tq, tq), 0) <= lax.broadcasted_iota(jnp.int32, (tq, tq), 1))
    outs = []
    for hh in range(2):
        qm = jnp.where(q_lane // dh == hh, q, jnp.zeros_like(q))
        cq = crow_ref[0, pl.ds(2 * pair + hh, 1), :]

        def block(kb, carry, diag):
            m, l, acc = carry
            k0 = pl.multiple_of(kb * tq, tq)
            st = _mm_nt(k_ref[0, pl.ds(k0, tq), :], qm)
            t = st - jnp.tile(cb[hh, pl.ds(k0, tq), :], (1, n_rep))
            if diag:
                t = jnp.where(causal, t, -jnp.inf)
            m_new = jnp.maximum(m, jnp.max(t, axis=0, keepdims=True) + cq)
            alpha = jnp.exp2(m - m_new)
            p = jnp.exp2(t + (cq - m_new))
            l = alpha * l + jnp.sum(p, axis=0, keepdims=True)
            v = vt_ref[0, hh * dh:(hh + 1) * dh, pl.ds(k0, tq)]
            acc = alpha * acc + jnp.dot(v, p.astype(BF16), preferred_element_type=F32)
            return m_new, l, acc

        init = (jnp.full((1, tq), -jnp.inf, F32), jnp.zeros((1, tq), F32), jnp.zeros((dh, tq), F32))
        carry = lax.fori_loop(0, qi, functools.partial(block, diag=False), init)
        m, l, acc = block(qi, carry, True)
        outs.append(acc / l)
    o_ref[0] = jnp.concatenate(outs, axis=0).T.astype(BF16)


def _fox_attn(q, k, vt, ccol, crow):
    B, S, D = q.shape
    return pl.pallas_call(
        _fox_kernel,
        grid=(B, D // LANES, S // TQ),
        in_specs=[
            pl.BlockSpec((1, TQ, LANES), lambda b, p, i: (b, i, p)),
            pl.BlockSpec((1, S, LANES), lambda b, p, i: (b, 0, p)),
            pl.BlockSpec((1, LANES, S), lambda b, p, i: (b, p, 0)),
            pl.BlockSpec((1, S, LANES), lambda b, p, i: (b, 0, 0)),
            pl.BlockSpec((1, FOX_HEADS, TQ), lambda b, p, i: (b, 0, i)),
        ],
        out_specs=pl.BlockSpec((1, TQ, LANES), lambda b, p, i: (b, i, p)),
        out_shape=jax.ShapeDtypeStruct((B, S, D), BF16),
        scratch_shapes=[pltpu.VMEM((2, S, LANES), F32)],
        compiler_params=pltpu.CompilerParams(
            dimension_semantics=("arbitrary", "arbitrary", "arbitrary"), vmem_limit_bytes=VMEM_LIMIT),
        name="fox_attn",
    )(q, k, vt, ccol, crow)


def kernel(x, mem, norm_mix_g, norm_mem_g, norm_memsrc_g, norm_ff_g, mem_wq, mem_wk, mem_wv, mem_wo, ff_w1, ff_w2,
           conv_pw1_w, conv_pw1_b, conv_dw_w, conv_dw_b, conv_ln_g, conv_ln_b, conv_pw2_w, conv_pw2_b,
           kv_norm_g, kvf_w, fgate_b, fox_wq, fox_wo, final_norm_g):
    D = x.shape[2]
    depth = norm_mix_g.shape[0]
    n_conv = conv_pw1_w.shape[0]
    bf = lambda w: w.astype(BF16)
    mem_wq_b, mem_wk_b, mem_wv_b, mem_wo_b = bf(mem_wq), bf(mem_wk), bf(mem_wv), bf(mem_wo)
    ff_w1_b, ff_w2_b = bf(ff_w1), bf(ff_w2)
    pw1_b, pw2_b = bf(conv_pw1_w), bf(conv_pw2_w)
    fox_wq_b, fox_wo_b = bf(fox_wq), bf(fox_wo)
    kvf_b = bf(kvf_w)
    wk_b = kvf_b[:, :D]
    wvt_b = kvf_b[:, D:2 * D].T
    wf_b = jnp.pad(kvf_b[:, 2 * D:], ((0, 0), (0, LANES - FOX_HEADS)))
    fb = jnp.pad(fgate_b, (0, LANES - FOX_HEADS)).reshape(1, LANES)

    assert depth == 2 and n_conv == 1 and fox_wq.shape[0] == 1

    mk, mv = _mem_kv(mem, norm_memsrc_g, mem_wk_b, mem_wv_b)
    tail = functools.partial(_tail, gmem=norm_mem_g, wq=mem_wq_b, k=mk, v=mv, wo=mem_wo_b,
                             gff=norm_ff_g, w1=ff_w1_b, w2=ff_w2_b)

    x = _conv_mix(x, norm_mix_g[0], pw1_b[0], conv_pw1_b[0], conv_dw_w[0], conv_dw_b[0],
                  conv_ln_g[0], conv_ln_b[0], pw2_b[0], conv_pw2_b[0])
    x = tail(x, 0)
    k, vt, q, ccol, crow = _kvfq(x, kv_norm_g, norm_mix_g[1], wk_b, wvt_b, wf_b, fb, fox_wq_b, 0)
    o = _fox_attn(q, k, vt, ccol, crow)
    return tail(x, 1, attn=(o, fox_wo_b, 0), gfin=final_norm_g)
```

```python
import functools
import math

import jax
import jax.numpy as jnp
import numpy as np
from jax import lax
from jax.experimental import pallas as pl
from jax.experimental.pallas import tpu as pltpu

F32 = jnp.float32
BF16 = jnp.bfloat16

CONV_WIDTH = 31
FOX_HEADS = 16
MEM_HEADS = 4
RMS_EPS = 1e-6
LN_EPS = 1e-5
LOG2E = 1.4426950408889634

LANES = 128
SUBLANES = 8
HALO = 32
TM = 512
TQ = 512
VMEM_LIMIT = 56 * 1024 * 1024


def _rms(x, g):
    return x * lax.rsqrt(jnp.mean(x * x, axis=-1, keepdims=True) + RMS_EPS) * g


def _mm(a, w):
    return jnp.dot(a.astype(BF16), w, preferred_element_type=F32)


def _mm_nt(a, b):
    return lax.dot_general(a, b, (((1,), (1,)), ((), ())), preferred_element_type=F32)


def _resident(shape, index_map):
    return pl.BlockSpec(shape, index_map, pipeline_mode=pl.Buffered(1))


def _mem_kv_kernel(mem_ref, g_ref, wk_ref, wv_ref, k_ref, v_ref):
    mn = _rms(mem_ref[0], g_ref[0]).astype(BF16)
    k_ref[0, 0] = jnp.dot(mn, wk_ref[0], preferred_element_type=F32).astype(BF16)
    v_ref[0, 0] = jnp.dot(mn, wv_ref[0], preferred_element_type=F32).astype(BF16)


def _mem_kv(mem, g, wk, wv):
    B, M, D = mem.shape
    L = wk.shape[0]
    out = jax.ShapeDtypeStruct((L, B, M, D), BF16)
    return pl.pallas_call(
        _mem_kv_kernel,
        grid=(L, B),
        in_specs=[
            pl.BlockSpec((1, M, D), lambda l, b: (b, 0, 0)),
            pl.BlockSpec((1, 1, D), lambda l, b: (l, 0, 0)),
            pl.BlockSpec((1, D, D), lambda l, b: (l, 0, 0)),
            pl.BlockSpec((1, D, D), lambda l, b: (l, 0, 0)),
        ],
        out_specs=[
            pl.BlockSpec((1, 1, M, D), lambda l, b: (l, b, 0, 0)),
            pl.BlockSpec((1, 1, M, D), lambda l, b: (l, b, 0, 0)),
        ],
        out_shape=[out, out],
        compiler_params=pltpu.CompilerParams(
            dimension_semantics=("arbitrary", "arbitrary"), vmem_limit_bytes=VMEM_LIMIT),
        name="mem_kv",
    )(mem, g.reshape(L, 1, D), wk, wv)


CONV_ROWS = 64
CONV_COLS = LANES


def _conv_mix_kernel(x_ref, g_ref, w1_ref, b1_ref, wdw_ref, bdw_ref, lng_ref, lnb_ref, w2_ref, b2_ref,
                     o_ref, ubuf, cbuf):
    tm, d = x_ref.shape[1], x_ref.shape[2]

    @pl.when(pl.program_id(1) == 0)
    def _():
        ubuf[0:HALO, :] = jnp.zeros((HALO, d), F32)

    x = x_ref[0]
    h = _rms(x, g_ref[...])
    u = _mm(h, w1_ref[...]) + b1_ref[...]
    ubuf[HALO:HALO + tm, :] = u[:, :d] * jax.nn.sigmoid(u[:, d:])

    base = HALO - (CONV_WIDTH - 1)
    n_col = d // CONV_COLS

    def chunk(i, carry):
        r0 = pl.multiple_of((i // n_col) * CONV_ROWS, CONV_ROWS)
        c0 = pl.multiple_of((i % n_col) * CONV_COLS, CONV_COLS)
        win = ubuf[pl.ds(r0, CONV_ROWS + HALO), pl.ds(c0, CONV_COLS)]
        acc = jnp.broadcast_to(bdw_ref[:, pl.ds(c0, CONV_COLS)], (CONV_ROWS, CONV_COLS))
        for r in range(SUBLANES):
            taps = [o for o in range(base, base + CONV_WIDTH) if o % SUBLANES == r]
            hi = max(taps) - r + CONV_ROWS
            sh = win[r:r + hi]
            for o in taps:
                acc = acc + wdw_ref[o - base:o - base + 1, pl.ds(c0, CONV_COLS)] * sh[o - r:o - r + CONV_ROWS]
        cbuf[pl.ds(r0, CONV_ROWS), pl.ds(c0, CONV_COLS)] = acc
        return carry

    lax.fori_loop(0, (tm // CONV_ROWS) * n_col, chunk, 0)
    ubuf[0:HALO, :] = ubuf[tm:tm + HALO, :]

    c = cbuf[...]
    mu = jnp.mean(c, axis=-1, keepdims=True)
    cc = c - mu
    var = jnp.mean(cc * cc, axis=-1, keepdims=True)
    y = cc * lax.rsqrt(var + LN_EPS) * lng_ref[...] + lnb_ref[...]
    y = y * jax.nn.sigmoid(y)
    o_ref[0] = x + _mm(y, w2_ref[...]) + b2_ref[...]


def _conv_mix(x, g, w1, b1, wdw, bdw, lng, lnb, w2, b2):
    B, S, D = x.shape
    row = lambda v: v.reshape(1, -1)
    const = lambda b, s: (0, 0)
    return pl.pallas_call(
        _conv_mix_kernel,
        grid=(B, S // TM),
        in_specs=[
            pl.BlockSpec((1, TM, D), lambda b, s: (b, s, 0)),
            _resident((1, D), const),
            _resident((D, 2 * D), const),
            _resident((1, 2 * D), const),
            _resident((CONV_WIDTH, D), const),
            _resident((1, D), const),
            _resident((1, D), const),
            _resident((1, D), const),
            _resident((D, D), const),
            _resident((1, D), const),
        ],
        out_specs=pl.BlockSpec((1, TM, D), lambda b, s: (b, s, 0)),
        out_shape=jax.ShapeDtypeStruct((B, S, D), F32),
        scratch_shapes=[pltpu.VMEM((TM + HALO, D), F32), pltpu.VMEM((TM, D), F32)],
        compiler_params=pltpu.CompilerParams(
            dimension_semantics=("arbitrary", "arbitrary"), vmem_limit_bytes=VMEM_LIMIT),
        name="conv_mix",
    )(x, row(g), w1, row(b1), wdw, row(bdw), row(lng), row(lnb), w2, row(b2))


FF_CHUNK = 1024


def _tail_kernel(*refs, has_attn, final):
    it = iter(refs)
    x_ref = next(it)
    if has_attn:
        oin_ref, fwo_ref = next(it), next(it)
    gmem_ref, wq_ref, k_ref, v_ref, wo_ref, gff_ref, w1_ref, w2_ref = (next(it) for _ in range(8))
    if final:
        gfin_ref = next(it)
    out_ref = next(it)

    d = x_ref.shape[2]
    dh = d // MEM_HEADS
    x = x_ref[0]
    if has_attn:
        x = x + jnp.dot(oin_ref[0], fwo_ref[0], preferred_element_type=F32)

    q = (_mm(_rms(x, gmem_ref[0]), wq_ref[0]) * (1.0 / math.sqrt(dh))).astype(BF16)
    heads = []
    for hd in range(MEM_HEADS):
        sl = slice(hd * dh, (hd + 1) * dh)
        s = _mm_nt(q[:, sl], k_ref[0, 0, :, sl])
        p = jnp.exp(s - jnp.max(s, axis=-1, keepdims=True))
        l = jnp.sum(p, axis=-1, keepdims=True)
        heads.append(_mm(p, v_ref[0, 0, :, sl]) / l)
    x = x + _mm(jnp.concatenate(heads, axis=-1), wo_ref[0])

    hb = _rms(x, gff_ref[0]).astype(BF16)
    acc = jnp.zeros_like(x)
    for c in range(w1_ref.shape[2] // FF_CHUNK):
        sl = slice(c * FF_CHUNK, (c + 1) * FF_CHUNK)
        a = jnp.maximum(jnp.dot(hb, w1_ref[0, :, sl], preferred_element_type=F32), 0.0)
        acc = acc + _mm(a * a, w2_ref[0, sl, :])
    x = x + acc
    if final:
        x = _rms(x, gfin_ref[...])
    out_ref[0] = x


def _tail(x, layer, gmem, wq, k, v, wo, gff, w1, w2, attn=None, gfin=None):
    B, S, D = x.shape
    L = wq.shape[0]
    M = k.shape[2]
    F = w1.shape[2]
    lay = lambda b, s: (layer, 0, 0)
    tile = pl.BlockSpec((1, TM, D), lambda b, s: (b, s, 0))
    args, specs = [x], [tile]
    if attn is not None:
        o_in, fwo, fl = attn
        args += [o_in, fwo]
        specs += [pl.BlockSpec((1, TM, D), lambda b, s: (b, s, 0)),
                  _resident((1, D, D), lambda b, s: (fl, 0, 0))]
    args += [gmem.reshape(L, 1, D), wq, k, v, wo, gff.reshape(L, 1, D), w1, w2]
    specs += [
        _resident((1, 1, D), lay),
        _resident((1, D, D), lay),
        pl.BlockSpec((1, 1, M, D), lambda b, s: (layer, b, 0, 0)),
        pl.BlockSpec((1, 1, M, D), lambda b, s: (layer, b, 0, 0)),
        _resident((1, D, D), lay),
        _resident((1, 1, D), lay),
        _resident((1, D, F), lay),
        _resident((1, F, D), lay),
    ]
    if gfin is not None:
        args.append(gfin.reshape(1, D))
        specs.append(_resident((1, D), lambda b, s: (0, 0)))
    return pl.pallas_call(
        functools.partial(_tail_kernel, has_attn=attn is not None, final=gfin is not None),
        grid=(B, S // TM),
        in_specs=specs,
        out_specs=tile,
        out_shape=jax.ShapeDtypeStruct((B, S, D), F32),
        compiler_params=pltpu.CompilerParams(
            dimension_semantics=("arbitrary", "arbitrary"), vmem_limit_bytes=VMEM_LIMIT),
        name=f"tail{layer}",
    )(*args)


VT_ROWS = 80
C_PIECES = 3


def _kvfq_kernel(x_ref, gkv_ref, gq_ref, wk_ref, wvt_ref, ones_ref, wf_ref, fb_ref, place_ref, wq_ref,
                 k_ref, kc_ref, vt_ref, q_ref, crow_ref, carry):
    tm = x_ref.shape[1]
    dh = x_ref.shape[2] // FOX_HEADS

    @pl.when(pl.program_id(1) == 0)
    def _():
        carry[...] = jnp.zeros_like(carry)

    x = x_ref[0]
    hk = _rms(x, gkv_ref[...]).astype(BF16)
    k_ref[0] = jnp.dot(hk, wk_ref[...], preferred_element_type=F32).astype(BF16)
    vt_ref[0] = (_mm_nt(wvt_ref[...], hk) + jnp.tile(ones_ref[...], (1, tm // LANES))).astype(BF16)

    f = jnp.dot(hk, wf_ref[...], preferred_element_type=F32) + fb_ref[...]
    log_f = jnp.minimum(f, 0.0) - jnp.log(1.0 + jnp.exp(-jnp.abs(f)))
    tri = (lax.broadcasted_iota(jnp.int32, (tm, tm), 1)
           <= lax.broadcasted_iota(jnp.int32, (tm, tm), 0)).astype(F32)
    c = jnp.dot(tri, log_f, preferred_element_type=F32, precision=lax.Precision.HIGHEST) + carry[0:1, :]
    carry[0:1, :] = c[tm - 1:tm, :]
    c2 = c * LOG2E
    crow_ref[0] = c2.T[0:FOX_HEADS, :]

    pieces, rest = [], c2
    for _ in range(C_PIECES):
        piece = rest.astype(BF16)
        pieces.append(piece)
        rest = rest - piece.astype(F32)
    kc_ref[0] = jnp.dot(jnp.concatenate(pieces, axis=1), place_ref[...],
                        preferred_element_type=F32).astype(BF16)

    hq = _rms(x, gq_ref[...])
    q_ref[0] = (_mm(hq, wq_ref[0]) * (LOG2E / math.sqrt(dh))).astype(BF16)


def _gate_placement(n_heads):
    place = np.zeros((C_PIECES * LANES, n_heads // 2 * LANES), np.float32)
    for h in range(n_heads):
        for i in range(C_PIECES):
            place[i * LANES + h, (h // 2) * LANES + (h % 2) * C_PIECES + i] = 1.0
    return jnp.asarray(place, BF16)


def _kvfq(x, gkv, gq, wk, wvt, vt_ones, wf, fb, wq, layer):
    place = _gate_placement(FOX_HEADS)
    B, S, D = x.shape
    R = wvt.shape[0]
    const = lambda b, s: (0, 0)
    tile = pl.BlockSpec((1, TM, D), lambda b, s: (b, s, 0))
    return pl.pallas_call(
        _kvfq_kernel,
        grid=(B, S // TM),
        in_specs=[
            tile,
            _resident((1, D), const),
            _resident((1, D), const),
            _resident((D, D), const),
            _resident((R, D), const),
            _resident((R, LANES), const),
            _resident((D, LANES), const),
            _resident((1, LANES), const),
            _resident(place.shape, const),
            _resident((1, D, D), lambda b, s: (layer, 0, 0)),
        ],
        out_specs=[
            tile,
            tile,
            pl.BlockSpec((1, R, TM), lambda b, s: (b, 0, s)),
            tile,
            pl.BlockSpec((1, FOX_HEADS, TM), lambda b, s: (b, 0, s)),
        ],
        out_shape=[
            jax.ShapeDtypeStruct((B, S, D), BF16),
            jax.ShapeDtypeStruct((B, S, D), BF16),
            jax.ShapeDtypeStruct((B, R, S), BF16),
            jax.ShapeDtypeStruct((B, S, D), BF16),
            jax.ShapeDtypeStruct((B, FOX_HEADS, S), F32),
        ],
        scratch_shapes=[pltpu.VMEM((8, LANES), F32)],
        compiler_params=pltpu.CompilerParams(
            dimension_semantics=("arbitrary", "arbitrary"), vmem_limit_bytes=VMEM_LIMIT),
        name="kvfq",
    )(x, gkv.reshape(1, D), gq.reshape(1, D), wk, wvt, vt_ones, wf, fb, place, wq)


FOX_GROUP = 4


def _fox_kernel(q_ref, k_ref, kc_ref, vt_ref, crow_ref, o_ref, tbuf, acc):
    tq = q_ref.shape[1]
    dh = LANES // 2
    grp = pl.program_id(1)
    qi = pl.program_id(2)
    G = FOX_GROUP

    lane = lax.broadcasted_iota(jnp.int32, (tq, LANES), 1)
    causal = (lax.broadcasted_iota(jnp.int32, (tq, tq), 0) <= lax.broadcasted_iota(jnp.int32, (tq, tq), 1))
    qx, cqs = [], []
    for h in range(G):
        pr, hh = divmod(h, 2)
        qp = q_ref[0, :, pr * LANES:(pr + 1) * LANES]
        gate = (lane >= hh * C_PIECES) & (lane < (hh + 1) * C_PIECES)
        qx.append(jnp.concatenate([jnp.where(lane // dh == hh, qp, jnp.zeros_like(qp)),
                                   jnp.where(gate, -1.0, 0.0).astype(BF16)], axis=1))
        cqs.append(crow_ref[0, pl.ds(G * grp + h, 1), :])
        acc[h] = jnp.zeros(acc.shape[1:], F32)

    def scores(kb, h, diag):
        pr = h // 2
        k0 = pl.multiple_of(kb * tq, tq)
        kx = jnp.concatenate([k_ref[0, pl.ds(k0, tq), pr * LANES:(pr + 1) * LANES],
                              kc_ref[0, pl.ds(k0, tq), pr * LANES:(pr + 1) * LANES]], axis=1)
        t = _mm_nt(kx, qx[h])
        if diag:
            t = jnp.where(causal, t, -jnp.inf)
        tbuf[h] = t
        return jnp.max(t, axis=0, keepdims=True)

    def consume(kb, h, mb, m):
        k0 = pl.multiple_of(kb * tq, tq)
        m_new = jnp.maximum(m, mb + cqs[h])
        p = jnp.exp2(tbuf[h] + (cqs[h] - m_new)).astype(BF16)
        v = vt_ref[0, h * VT_ROWS:(h + 1) * VT_ROWS, pl.ds(k0, tq)]
        acc[h] = jnp.exp2(m - m_new) * acc[h] + jnp.dot(v, p, preferred_element_type=F32)
        return m_new

    def sweep(kb, kb_prev, mb_last, ms, diag):
        ms = list(ms)
        mb = scores(kb, 0, diag)
        if mb_last is not None:
            ms[G - 1] = consume(kb_prev, G - 1, mb_last, ms[G - 1])
        for h in range(1, G):
            mb_next = scores(kb, h, diag)
            ms[h - 1] = consume(kb, h - 1, mb, ms[h - 1])
            mb = mb_next
        return mb, tuple(ms)

    def step(j, state):
        mb_last, ms = state
        return sweep(j, jnp.where(j == 0, qi, j - 1), mb_last, ms, False)

    ms = (jnp.full((1, tq), -jnp.inf, F32),) * G
    mb_last, ms = lax.fori_loop(0, qi, step, sweep(qi, None, None, ms, True))
    consume(jnp.where(qi == 0, qi, qi - 1), G - 1, mb_last, ms[G - 1])
    o_ref[0] = jnp.concatenate([acc[h, 0:dh] / acc[h, dh:dh + 1] for h in range(G)], axis=0).T.astype(BF16)


def _fox_attn(q, k, kc, vt, crow):
    B, S, D = q.shape
    G = FOX_GROUP
    W = G * (LANES // 2)
    return pl.pallas_call(
        _fox_kernel,
        grid=(B, D // W, S // TQ),
        in_specs=[
            pl.BlockSpec((1, TQ, W), lambda b, g, i: (b, i, g)),
            pl.BlockSpec((1, S, W), lambda b, g, i: (b, 0, g)),
            pl.BlockSpec((1, S, W), lambda b, g, i: (b, 0, g)),
            pl.BlockSpec((1, G * VT_ROWS, S), lambda b, g, i: (b, g, 0)),
            pl.BlockSpec((1, FOX_HEADS, TQ), lambda b, g, i: (b, 0, i)),
        ],
        out_specs=pl.BlockSpec((1, TQ, W), lambda b, g, i: (b, i, g)),
        out_shape=jax.ShapeDtypeStruct((B, S, D), BF16),
        scratch_shapes=[pltpu.VMEM((G, TQ, TQ), F32), pltpu.VMEM((G, VT_ROWS, TQ), F32)],
        compiler_params=pltpu.CompilerParams(
            dimension_semantics=("arbitrary", "arbitrary", "arbitrary"), vmem_limit_bytes=VMEM_LIMIT),
        name="fox_attn",
    )(q, k, kc, vt, crow)


def kernel(x, mem, norm_mix_g, norm_mem_g, norm_memsrc_g, norm_ff_g, mem_wq, mem_wk, mem_wv, mem_wo, ff_w1, ff_w2,
           conv_pw1_w, conv_pw1_b, conv_dw_w, conv_dw_b, conv_ln_g, conv_ln_b, conv_pw2_w, conv_pw2_b,
           kv_norm_g, kvf_w, fgate_b, fox_wq, fox_wo, final_norm_g):
    D = x.shape[2]
    depth = norm_mix_g.shape[0]
    n_conv = conv_pw1_w.shape[0]
    bf = lambda w: w.astype(BF16)
    mem_wq_b, mem_wk_b, mem_wv_b, mem_wo_b = bf(mem_wq), bf(mem_wk), bf(mem_wv), bf(mem_wo)
    ff_w1_b, ff_w2_b = bf(ff_w1), bf(ff_w2)
    pw1_b, pw2_b = bf(conv_pw1_w), bf(conv_pw2_w)
    fox_wq_b, fox_wo_b = bf(fox_wq), bf(fox_wo)
    kvf_b = bf(kvf_w)
    wk_b = kvf_b[:, :D]
    dh = D // FOX_HEADS
    wvt_b = jnp.pad(kvf_b[:, D:2 * D].T.reshape(FOX_HEADS, dh, D),
                    ((0, 0), (0, VT_ROWS - dh), (0, 0))).reshape(FOX_HEADS * VT_ROWS, D)
    vt_ones = jnp.tile((jnp.arange(VT_ROWS) >= dh).astype(F32)[:, None], (FOX_HEADS, LANES))
    wf_b = jnp.pad(kvf_b[:, 2 * D:], ((0, 0), (0, LANES - FOX_HEADS)))
    fb = jnp.pad(fgate_b, (0, LANES - FOX_HEADS)).reshape(1, LANES)

    assert depth == 2 and n_conv == 1 and fox_wq.shape[0] == 1

    mk, mv = _mem_kv(mem, norm_memsrc_g, mem_wk_b, mem_wv_b)
    tail = functools.partial(_tail, gmem=norm_mem_g, wq=mem_wq_b, k=mk, v=mv, wo=mem_wo_b,
                             gff=norm_ff_g, w1=ff_w1_b, w2=ff_w2_b)

    x = _conv_mix(x, norm_mix_g[0], pw1_b[0], conv_pw1_b[0], conv_dw_w[0], conv_dw_b[0],
                  conv_ln_g[0], conv_ln_b[0], pw2_b[0], conv_pw2_b[0])
    x = tail(x, 0)
    k, kc, vt, q, crow = _kvfq(x, kv_norm_g, norm_mix_g[1], wk_b, wvt_b, vt_ones, wf_b, fb, fox_wq_b, 0)
    o = _fox_attn(q, k, kc, vt, crow)
    return tail(x, 1, attn=(o, fox_wo_b, 0), gfin=final_norm_g)
```

```python
import functools
import math

import jax
import jax.numpy as jnp
import numpy as np
from jax import lax
from jax.experimental import pallas as pl
from jax.experimental.pallas import tpu as pltpu

F32 = jnp.float32
BF16 = jnp.bfloat16

CONV_WIDTH = 31
FOX_HEADS = 16
MEM_HEADS = 4
RMS_EPS = 1e-6
LN_EPS = 1e-5
LOG2E = 1.4426950408889634

LANES = 128
SUBLANES = 8
HALO = 32
TM = 512
TQ = 512
TK = 256
VMEM_LIMIT = 56 * 1024 * 1024


def _rms(x, g):
    return x * lax.rsqrt(jnp.mean(x * x, axis=-1, keepdims=True) + RMS_EPS) * g


def _mm(a, w):
    return jnp.dot(a.astype(BF16), w, preferred_element_type=F32)


def _mm_nt(a, b):
    return lax.dot_general(a, b, (((1,), (1,)), ((), ())), preferred_element_type=F32)


def _resident(shape, index_map):
    return pl.BlockSpec(shape, index_map, pipeline_mode=pl.Buffered(1))


def _mem_kv_kernel(mem_ref, g_ref, wk_ref, wv_ref, k_ref, v_ref):
    mn = _rms(mem_ref[0], g_ref[0]).astype(BF16)
    k_ref[0, 0] = jnp.dot(mn, wk_ref[0], preferred_element_type=F32).astype(BF16)
    v_ref[0, 0] = jnp.dot(mn, wv_ref[0], preferred_element_type=F32).astype(BF16)


def _mem_kv(mem, g, wk, wv):
    B, M, D = mem.shape
    L = wk.shape[0]
    out = jax.ShapeDtypeStruct((L, B, M, D), BF16)
    return pl.pallas_call(
        _mem_kv_kernel,
        grid=(L, B),
        in_specs=[
            pl.BlockSpec((1, M, D), lambda l, b: (b, 0, 0)),
            pl.BlockSpec((1, 1, D), lambda l, b: (l, 0, 0)),
            pl.BlockSpec((1, D, D), lambda l, b: (l, 0, 0)),
            pl.BlockSpec((1, D, D), lambda l, b: (l, 0, 0)),
        ],
        out_specs=[
            pl.BlockSpec((1, 1, M, D), lambda l, b: (l, b, 0, 0)),
            pl.BlockSpec((1, 1, M, D), lambda l, b: (l, b, 0, 0)),
        ],
        out_shape=[out, out],
        compiler_params=pltpu.CompilerParams(
            dimension_semantics=("arbitrary", "arbitrary"), vmem_limit_bytes=VMEM_LIMIT),
        name="mem_kv",
    )(mem, g.reshape(L, 1, D), wk, wv)


CONV_ROWS = 128
CONV_COLS = LANES


def _conv_mix_kernel(x_ref, g_ref, w1_ref, b1_ref, wdw_ref, bdw_ref, lng_ref, lnb_ref, w2_ref, b2_ref,
                     o_ref, ubuf, cbuf):
    tm, d = x_ref.shape[1], x_ref.shape[2]

    @pl.when(pl.program_id(1) == 0)
    def _():
        ubuf[0:HALO, :] = jnp.zeros((HALO, d), F32)

    x = x_ref[0]
    h = _rms(x, g_ref[...])
    u = _mm(h, w1_ref[...]) + b1_ref[...]
    ubuf[HALO:HALO + tm, :] = u[:, :d] * jax.nn.sigmoid(u[:, d:])

    base = HALO - (CONV_WIDTH - 1)
    n_col = d // CONV_COLS

    def chunk(i, carry):
        r0 = pl.multiple_of((i // n_col) * CONV_ROWS, CONV_ROWS)
        c0 = pl.multiple_of((i % n_col) * CONV_COLS, CONV_COLS)
        win = ubuf[pl.ds(r0, CONV_ROWS + HALO), pl.ds(c0, CONV_COLS)]
        acc = jnp.broadcast_to(bdw_ref[:, pl.ds(c0, CONV_COLS)], (CONV_ROWS, CONV_COLS))
        for r in range(SUBLANES):
            taps = [o for o in range(base, base + CONV_WIDTH) if o % SUBLANES == r]
            sh = win if r == 0 else pltpu.roll(win, CONV_ROWS + HALO - r, axis=0)
            for o in taps:
                acc = acc + wdw_ref[o - base:o - base + 1, pl.ds(c0, CONV_COLS)] * sh[o - r:o - r + CONV_ROWS]
        cbuf[pl.ds(r0, CONV_ROWS), pl.ds(c0, CONV_COLS)] = acc
        return carry

    lax.fori_loop(0, (tm // CONV_ROWS) * n_col, chunk, 0)
    ubuf[0:HALO, :] = ubuf[tm:tm + HALO, :]

    c = cbuf[...]
    mu = jnp.mean(c, axis=-1, keepdims=True)
    cc = c - mu
    var = jnp.mean(cc * cc, axis=-1, keepdims=True)
    y = cc * lax.rsqrt(var + LN_EPS) * lng_ref[...] + lnb_ref[...]
    y = y * jax.nn.sigmoid(y)
    o_ref[0] = x + _mm(y, w2_ref[...]) + b2_ref[...]


def _conv_mix(x, g, w1, b1, wdw, bdw, lng, lnb, w2, b2):
    B, S, D = x.shape
    row = lambda v: v.reshape(1, -1)
    const = lambda b, s: (0, 0)
    return pl.pallas_call(
        _conv_mix_kernel,
        grid=(B, S // TM),
        in_specs=[
            pl.BlockSpec((1, TM, D), lambda b, s: (b, s, 0)),
            _resident((1, D), const),
            _resident((D, 2 * D), const),
            _resident((1, 2 * D), const),
            _resident((CONV_WIDTH, D), const),
            _resident((1, D), const),
            _resident((1, D), const),
            _resident((1, D), const),
            _resident((D, D), const),
            _resident((1, D), const),
        ],
        out_specs=pl.BlockSpec((1, TM, D), lambda b, s: (b, s, 0)),
        out_shape=jax.ShapeDtypeStruct((B, S, D), F32),
        scratch_shapes=[pltpu.VMEM((TM + HALO, D), F32), pltpu.VMEM((TM, D), F32)],
        compiler_params=pltpu.CompilerParams(
            dimension_semantics=("arbitrary", "arbitrary"), vmem_limit_bytes=VMEM_LIMIT),
        name="conv_mix",
    )(x, row(g), w1, row(b1), wdw, row(bdw), row(lng), row(lnb), w2, row(b2))


FF_CHUNK = 1024


def _tail_kernel(*refs, has_attn, final):
    it = iter(refs)
    x_ref = next(it)
    if has_attn:
        oin_ref, fwo_ref = next(it), next(it)
    gmem_ref, wq_ref, k_ref, v_ref, wo_ref, gff_ref, w1_ref, w2_ref = (next(it) for _ in range(8))
    if final:
        gfin_ref = next(it)
    out_ref = next(it)

    d = x_ref.shape[2]
    dh = d // MEM_HEADS
    x = x_ref[0]
    if has_attn:
        x = x + jnp.dot(oin_ref[0], fwo_ref[0], preferred_element_type=F32)

    q = (_mm(_rms(x, gmem_ref[0]), wq_ref[0]) * (1.0 / math.sqrt(dh))).astype(BF16)
    heads = []
    for hd in range(MEM_HEADS):
        sl = slice(hd * dh, (hd + 1) * dh)
        s = _mm_nt(q[:, sl], k_ref[0, 0, :, sl])
        p = jnp.exp(s - jnp.max(s, axis=-1, keepdims=True))
        l = jnp.sum(p, axis=-1, keepdims=True)
        heads.append(_mm(p, v_ref[0, 0, :, sl]) / l)
    x = x + _mm(jnp.concatenate(heads, axis=-1), wo_ref[0])

    hb = _rms(x, gff_ref[0]).astype(BF16)
    acc = jnp.zeros_like(x)
    for c in range(w1_ref.shape[2] // FF_CHUNK):
        sl = slice(c * FF_CHUNK, (c + 1) * FF_CHUNK)
        a = jnp.maximum(jnp.dot(hb, w1_ref[0, :, sl], preferred_element_type=F32), 0.0)
        acc = acc + _mm(a * a, w2_ref[0, sl, :])
    x = x + acc
    if final:
        x = _rms(x, gfin_ref[...])
    out_ref[0] = x


def _tail(x, layer, gmem, wq, k, v, wo, gff, w1, w2, attn=None, gfin=None):
    B, S, D = x.shape
    L = wq.shape[0]
    M = k.shape[2]
    F = w1.shape[2]
    lay = lambda b, s: (layer, 0, 0)
    tile = pl.BlockSpec((1, TM, D), lambda b, s: (b, s, 0))
    args, specs = [x], [tile]
    if attn is not None:
        o_in, fwo, fl = attn
        args += [o_in, fwo]
        specs += [pl.BlockSpec((1, TM, D), lambda b, s: (b, s, 0)),
                  _resident((1, D, D), lambda b, s: (fl, 0, 0))]
    args += [gmem.reshape(L, 1, D), wq, k, v, wo, gff.reshape(L, 1, D), w1, w2]
    specs += [
        _resident((1, 1, D), lay),
        _resident((1, D, D), lay),
        pl.BlockSpec((1, 1, M, D), lambda b, s: (layer, b, 0, 0)),
        pl.BlockSpec((1, 1, M, D), lambda b, s: (layer, b, 0, 0)),
        _resident((1, D, D), lay),
        _resident((1, 1, D), lay),
        _resident((1, D, F), lay),
        _resident((1, F, D), lay),
    ]
    if gfin is not None:
        args.append(gfin.reshape(1, D))
        specs.append(_resident((1, D), lambda b, s: (0, 0)))
    return pl.pallas_call(
        functools.partial(_tail_kernel, has_attn=attn is not None, final=gfin is not None),
        grid=(B, S // TM),
        in_specs=specs,
        out_specs=tile,
        out_shape=jax.ShapeDtypeStruct((B, S, D), F32),
        compiler_params=pltpu.CompilerParams(
            dimension_semantics=("arbitrary", "arbitrary"), vmem_limit_bytes=VMEM_LIMIT),
        name=f"tail{layer}",
    )(*args)


VT_ROWS = 80
C_PIECES = 3


def _kvfq_kernel(x_ref, gkv_ref, gq_ref, wk_ref, wvt_ref, ones_ref, wf_ref, fb_ref, place_ref, wq_ref,
                 k_ref, kc_ref, vt_ref, q_ref, crow_ref, carry):
    tm = x_ref.shape[1]
    dh = x_ref.shape[2] // FOX_HEADS

    @pl.when(pl.program_id(1) == 0)
    def _():
        carry[...] = jnp.zeros_like(carry)

    x = x_ref[0]
    hk = _rms(x, gkv_ref[...]).astype(BF16)
    k_ref[0] = jnp.dot(hk, wk_ref[...], preferred_element_type=F32).astype(BF16)
    vt_ref[0] = (_mm_nt(wvt_ref[...], hk) + jnp.tile(ones_ref[...], (1, tm // LANES))).astype(BF16)

    f = jnp.dot(hk, wf_ref[...], preferred_element_type=F32) + fb_ref[...]
    log_f = jnp.minimum(f, 0.0) - jnp.log(1.0 + jnp.exp(-jnp.abs(f)))
    tri = (lax.broadcasted_iota(jnp.int32, (tm, tm), 1)
           <= lax.broadcasted_iota(jnp.int32, (tm, tm), 0)).astype(F32)
    c = jnp.dot(tri, log_f, preferred_element_type=F32, precision=lax.Precision.HIGHEST) + carry[0:1, :]
    carry[0:1, :] = c[tm - 1:tm, :]
    c2 = c * LOG2E
    crow_ref[0] = c2.T[0:FOX_HEADS, :]

    pieces, rest = [], c2
    for _ in range(C_PIECES):
        piece = rest.astype(BF16)
        pieces.append(piece)
        rest = rest - piece.astype(F32)
    kc_ref[0] = jnp.dot(jnp.concatenate(pieces, axis=1), place_ref[...],
                        preferred_element_type=F32).astype(BF16)

    hq = _rms(x, gq_ref[...]).astype(BF16)
    q_ref[0] = (_mm_nt(wq_ref[0], hq) * (LOG2E / math.sqrt(dh))).astype(BF16)


def _gate_placement(n_heads):
    place = np.zeros((C_PIECES * LANES, n_heads // 2 * LANES), np.float32)
    for h in range(n_heads):
        for i in range(C_PIECES):
            place[i * LANES + h, (h // 2) * LANES + (h % 2) * C_PIECES + i] = 1.0
    return jnp.asarray(place, BF16)


def _kvfq(x, gkv, gq, wk, wvt, vt_ones, wf, fb, wq, layer):
    place = _gate_placement(FOX_HEADS)
    B, S, D = x.shape
    R = wvt.shape[0]
    const = lambda b, s: (0, 0)
    tile = pl.BlockSpec((1, TM, D), lambda b, s: (b, s, 0))
    return pl.pallas_call(
        _kvfq_kernel,
        grid=(B, S // TM),
        in_specs=[
            tile,
            _resident((1, D), const),
            _resident((1, D), const),
            _resident((D, D), const),
            _resident((R, D), const),
            _resident((R, LANES), const),
            _resident((D, LANES), const),
            _resident((1, LANES), const),
            _resident(place.shape, const),
            _resident((1, D, D), lambda b, s: (layer, 0, 0)),
        ],
        out_specs=[
            tile,
            tile,
            pl.BlockSpec((1, R, TM), lambda b, s: (b, 0, s)),
            pl.BlockSpec((1, D, TM), lambda b, s: (b, 0, s)),
            pl.BlockSpec((1, FOX_HEADS, TM), lambda b, s: (b, 0, s)),
        ],
        out_shape=[
            jax.ShapeDtypeStruct((B, S, D), BF16),
            jax.ShapeDtypeStruct((B, S, D), BF16),
            jax.ShapeDtypeStruct((B, R, S), BF16),
            jax.ShapeDtypeStruct((B, D, S), BF16),
            jax.ShapeDtypeStruct((B, FOX_HEADS, S), F32),
        ],
        scratch_shapes=[pltpu.VMEM((8, LANES), F32)],
        compiler_params=pltpu.CompilerParams(
            dimension_semantics=("arbitrary", "arbitrary"), vmem_limit_bytes=VMEM_LIMIT),
        name="kvfq",
    )(x, gkv.reshape(1, D), gq.reshape(1, D), wk, wvt, vt_ones, wf, fb, place, wq)


FOX_GROUP = 4
FOX_LAG = 2


def _fox_kernel(q_ref, k_ref, kc_ref, vt_ref, crow_ref, o_ref, tbuf, acc):
    tq = q_ref.shape[2]
    dh = LANES // 2
    grp = pl.program_id(1)
    qi = pl.program_id(2)
    G = FOX_GROUP

    tk = tbuf.shape[1]
    n_sub = tq // tk
    row = lax.broadcasted_iota(jnp.int32, (LANES, tq), 0)
    key_minus_query = (lax.broadcasted_iota(jnp.int32, (tk, tq), 0) - lax.broadcasted_iota(jnp.int32, (tk, tq), 1))
    qx, cqs = [], []
    for h in range(G):
        pr, hh = divmod(h, 2)
        qp = q_ref[0, pr * LANES:(pr + 1) * LANES, :]
        gate = (row >= hh * C_PIECES) & (row < (hh + 1) * C_PIECES)
        qx.append(jnp.concatenate([jnp.where(row // dh == hh, qp, jnp.zeros_like(qp)),
                                   jnp.where(gate, -1.0, 0.0).astype(BF16)], axis=0))
        cqs.append(crow_ref[0, pl.ds(G * grp + h, 1), :])
        acc[h] = jnp.zeros(acc.shape[1:], F32)

    def scores(kb, h, diag_offset):
        pr = h // 2
        k0 = pl.multiple_of(kb * tk, tk)
        kx = jnp.concatenate([k_ref[0, pl.ds(k0, tk), pr * LANES:(pr + 1) * LANES],
                              kc_ref[0, pl.ds(k0, tk), pr * LANES:(pr + 1) * LANES]], axis=1)
        t = jnp.dot(kx, qx[h], preferred_element_type=F32)
        if diag_offset is not None:
            t = jnp.where(key_minus_query <= -diag_offset, t, -jnp.inf)
        tbuf[h] = t
        return jnp.max(t, axis=0, keepdims=True)

    def consume(kb, h, mb, m):
        k0 = pl.multiple_of(kb * tk, tk)
        m_new = jnp.maximum(m, mb + cqs[h])
        p = jnp.exp2(tbuf[h] + (cqs[h] - m_new)).astype(BF16)
        v = vt_ref[0, h * VT_ROWS:(h + 1) * VT_ROWS, pl.ds(k0, tk)]
        acc[h] = jnp.exp2(m - m_new) * acc[h] + jnp.dot(v, p, preferred_element_type=F32)
        return m_new

    def sweep(kb, state, diag_offset=None):
        kb_prev, pending, ms = state
        ms, mbs = list(ms), []
        for h in range(G):
            mbs.append(scores(kb, h, diag_offset))
            if h >= FOX_LAG:
                ms[h - FOX_LAG] = consume(kb, h - FOX_LAG, mbs[h - FOX_LAG], ms[h - FOX_LAG])
            elif pending is not None:
                hp = h + G - FOX_LAG
                ms[hp] = consume(kb_prev, hp, pending[h], ms[hp])
        return kb, tuple(mbs[G - FOX_LAG:]), tuple(ms)

    def step(j, state):
        for i in range(n_sub):
            state = sweep(n_sub * j + i, state)
        return state

    state = (None, None, (jnp.full((1, tq), -jnp.inf, F32),) * G)
    for i in range(n_sub):
        state = sweep(n_sub * qi + i, state, diag_offset=i * tk)
    kb_prev, pending, ms = lax.fori_loop(0, qi, step, state)
    for h in range(FOX_LAG):
        hp = h + G - FOX_LAG
        consume(kb_prev, hp, pending[h], ms[hp])
    o_ref[0] = jnp.concatenate([acc[h, 0:dh] / acc[h, dh:dh + 1] for h in range(G)], axis=0).T.astype(BF16)


def _fox_attn(qt, k, kc, vt, crow):
    B, S, D = k.shape
    G = FOX_GROUP
    W = G * (LANES // 2)
    return pl.pallas_call(
        _fox_kernel,
        grid=(B, D // W, S // TQ),
        in_specs=[
            pl.BlockSpec((1, W, TQ), lambda b, g, i: (b, g, i)),
            pl.BlockSpec((1, S, W), lambda b, g, i: (b, 0, g)),
            pl.BlockSpec((1, S, W), lambda b, g, i: (b, 0, g)),
            pl.BlockSpec((1, G * VT_ROWS, S), lambda b, g, i: (b, g, 0)),
            pl.BlockSpec((1, FOX_HEADS, TQ), lambda b, g, i: (b, 0, i)),
        ],
        out_specs=pl.BlockSpec((1, TQ, W), lambda b, g, i: (b, i, g)),
        out_shape=jax.ShapeDtypeStruct((B, S, D), BF16),
        scratch_shapes=[pltpu.VMEM((G, TK, TQ), F32), pltpu.VMEM((G, VT_ROWS, TQ), F32)],
        compiler_params=pltpu.CompilerParams(
            dimension_semantics=("arbitrary", "arbitrary", "arbitrary"), vmem_limit_bytes=VMEM_LIMIT),
        name="fox_attn",
    )(qt, k, kc, vt, crow)


def kernel(x, mem, norm_mix_g, norm_mem_g, norm_memsrc_g, norm_ff_g, mem_wq, mem_wk, mem_wv, mem_wo, ff_w1, ff_w2,
           conv_pw1_w, conv_pw1_b, conv_dw_w, conv_dw_b, conv_ln_g, conv_ln_b, conv_pw2_w, conv_pw2_b,
           kv_norm_g, kvf_w, fgate_b, fox_wq, fox_wo, final_norm_g):
    D = x.shape[2]
    depth = norm_mix_g.shape[0]
    n_conv = conv_pw1_w.shape[0]
    bf = lambda w: w.astype(BF16)
    mem_wq_b, mem_wk_b, mem_wv_b, mem_wo_b = bf(mem_wq), bf(mem_wk), bf(mem_wv), bf(mem_wo)
    ff_w1_b, ff_w2_b = bf(ff_w1), bf(ff_w2)
    pw1_b, pw2_b = bf(conv_pw1_w), bf(conv_pw2_w)
    fox_wqt_b, fox_wo_b = bf(fox_wq).transpose(0, 2, 1), bf(fox_wo)
    kvf_b = bf(kvf_w)
    wk_b = kvf_b[:, :D]
    dh = D // FOX_HEADS
    wvt_b = jnp.pad(kvf_b[:, D:2 * D].T.reshape(FOX_HEADS, dh, D),
                    ((0, 0), (0, VT_ROWS - dh), (0, 0))).reshape(FOX_HEADS * VT_ROWS, D)
    vt_ones = jnp.tile((jnp.arange(VT_ROWS) >= dh).astype(F32)[:, None], (FOX_HEADS, LANES))
    wf_b = jnp.pad(kvf_b[:, 2 * D:], ((0, 0), (0, LANES - FOX_HEADS)))
    fb = jnp.pad(fgate_b, (0, LANES - FOX_HEADS)).reshape(1, LANES)

    assert depth == 2 and n_conv == 1 and fox_wq.shape[0] == 1

    mk, mv = _mem_kv(mem, norm_memsrc_g, mem_wk_b, mem_wv_b)
    tail = functools.partial(_tail, gmem=norm_mem_g, wq=mem_wq_b, k=mk, v=mv, wo=mem_wo_b,
                             gff=norm_ff_g, w1=ff_w1_b, w2=ff_w2_b)

    x = _conv_mix(x, norm_mix_g[0], pw1_b[0], conv_pw1_b[0], conv_dw_w[0], conv_dw_b[0],
                  conv_ln_g[0], conv_ln_b[0], pw2_b[0], conv_pw2_b[0])
    x = tail(x, 0)
    k, kc, vt, qt, crow = _kvfq(x, kv_norm_g, norm_mix_g[1], wk_b, wvt_b, vt_ones, wf_b, fb, fox_wqt_b, 0)
    o = _fox_attn(qt, k, kc, vt, crow)
    return tail(x, 1, attn=(o, fox_wo_b, 0), gfin=final_norm_g)
```

```python
import functools
import math

import jax
import jax.numpy as jnp
import numpy as np
from jax import lax
from jax.experimental import pallas as pl
from jax.experimental.pallas import tpu as pltpu

F32 = jnp.float32
BF16 = jnp.bfloat16

CONV_WIDTH = 31
FOX_HEADS = 16
MEM_HEADS = 4
RMS_EPS = 1e-6
LN_EPS = 1e-5
LOG2E = 1.4426950408889634

LANES = 128
SUBLANES = 8
HALO = 32
TM = 512
TQ = 512
TK = 256
VMEM_LIMIT = 56 * 1024 * 1024


def _rms(x, g):
    return x * lax.rsqrt(jnp.mean(x * x, axis=-1, keepdims=True) + RMS_EPS) * g


def _mm(a, w):
    return jnp.dot(a.astype(BF16), w, preferred_element_type=F32)


def _mm_nt(a, b):
    return lax.dot_general(a, b, (((1,), (1,)), ((), ())), preferred_element_type=F32)


def _resident(shape, index_map):
    return pl.BlockSpec(shape, index_map, pipeline_mode=pl.Buffered(1))


def _mem_kv_kernel(mem_ref, g_ref, wk_ref, wv_ref, k_ref, v_ref):
    mn = _rms(mem_ref[0], g_ref[0]).astype(BF16)
    k_ref[0, 0] = jnp.dot(mn, wk_ref[0], preferred_element_type=F32).astype(BF16)
    v_ref[0, 0] = jnp.dot(mn, wv_ref[0], preferred_element_type=F32).astype(BF16)


def _mem_kv(mem, g, wk, wv):
    B, M, D = mem.shape
    L = wk.shape[0]
    out = jax.ShapeDtypeStruct((L, B, M, D), BF16)
    return pl.pallas_call(
        _mem_kv_kernel,
        grid=(L, B),
        in_specs=[
            pl.BlockSpec((1, M, D), lambda l, b: (b, 0, 0)),
            pl.BlockSpec((1, 1, D), lambda l, b: (l, 0, 0)),
            pl.BlockSpec((1, D, D), lambda l, b: (l, 0, 0)),
            pl.BlockSpec((1, D, D), lambda l, b: (l, 0, 0)),
        ],
        out_specs=[
            pl.BlockSpec((1, 1, M, D), lambda l, b: (l, b, 0, 0)),
            pl.BlockSpec((1, 1, M, D), lambda l, b: (l, b, 0, 0)),
        ],
        out_shape=[out, out],
        compiler_params=pltpu.CompilerParams(
            dimension_semantics=("arbitrary", "arbitrary"), vmem_limit_bytes=VMEM_LIMIT),
        name="mem_kv",
    )(mem, g.reshape(L, 1, D), wk, wv)


CONV_ROWS = 128
CONV_COLS = LANES


def _conv_mix_kernel(x_ref, g_ref, w1_ref, b1_ref, wdw_ref, bdw_ref, lng_ref, lnb_ref, w2_ref, b2_ref,
                     o_ref, ubuf, cbuf):
    tm, d = x_ref.shape[1], x_ref.shape[2]

    @pl.when(pl.program_id(1) == 0)
    def _():
        ubuf[0:HALO, :] = jnp.zeros((HALO, d), F32)

    x = x_ref[0]
    h = _rms(x, g_ref[...])
    u = _mm(h, w1_ref[...]) + b1_ref[...]
    ubuf[HALO:HALO + tm, :] = u[:, :d] * jax.nn.sigmoid(u[:, d:])

    base = HALO - (CONV_WIDTH - 1)
    n_col = d // CONV_COLS

    def chunk(i, carry):
        r0 = pl.multiple_of((i // n_col) * CONV_ROWS, CONV_ROWS)
        c0 = pl.multiple_of((i % n_col) * CONV_COLS, CONV_COLS)
        win = ubuf[pl.ds(r0, CONV_ROWS + HALO), pl.ds(c0, CONV_COLS)]
        acc = jnp.broadcast_to(bdw_ref[:, pl.ds(c0, CONV_COLS)], (CONV_ROWS, CONV_COLS))
        for r in range(SUBLANES):
            taps = [o for o in range(base, base + CONV_WIDTH) if o % SUBLANES == r]
            sh = win if r == 0 else pltpu.roll(win, CONV_ROWS + HALO - r, axis=0)
            for o in taps:
                acc = acc + wdw_ref[o - base:o - base + 1, pl.ds(c0, CONV_COLS)] * sh[o - r:o - r + CONV_ROWS]
        cbuf[pl.ds(r0, CONV_ROWS), pl.ds(c0, CONV_COLS)] = acc
        return carry

    lax.fori_loop(0, (tm // CONV_ROWS) * n_col, chunk, 0)
    ubuf[0:HALO, :] = ubuf[tm:tm + HALO, :]

    c = cbuf[...]
    mu = jnp.mean(c, axis=-1, keepdims=True)
    cc = c - mu
    var = jnp.mean(cc * cc, axis=-1, keepdims=True)
    y = cc * lax.rsqrt(var + LN_EPS) * lng_ref[...] + lnb_ref[...]
    y = y * jax.nn.sigmoid(y)
    o_ref[0] = x + _mm(y, w2_ref[...]) + b2_ref[...]


def _conv_mix(x, g, w1, b1, wdw, bdw, lng, lnb, w2, b2):
    B, S, D = x.shape
    row = lambda v: v.reshape(1, -1)
    const = lambda b, s: (0, 0)
    return pl.pallas_call(
        _conv_mix_kernel,
        grid=(B, S // TM),
        in_specs=[
            pl.BlockSpec((1, TM, D), lambda b, s: (b, s, 0)),
            _resident((1, D), const),
            _resident((D, 2 * D), const),
            _resident((1, 2 * D), const),
            _resident((CONV_WIDTH, D), const),
            _resident((1, D), const),
            _resident((1, D), const),
            _resident((1, D), const),
            _resident((D, D), const),
            _resident((1, D), const),
        ],
        out_specs=pl.BlockSpec((1, TM, D), lambda b, s: (b, s, 0)),
        out_shape=jax.ShapeDtypeStruct((B, S, D), F32),
        scratch_shapes=[pltpu.VMEM((TM + HALO, D), F32), pltpu.VMEM((TM, D), F32)],
        compiler_params=pltpu.CompilerParams(
            dimension_semantics=("arbitrary", "arbitrary"), vmem_limit_bytes=VMEM_LIMIT),
        name="conv_mix",
    )(x, row(g), w1, row(b1), wdw, row(bdw), row(lng), row(lnb), w2, row(b2))


FF_CHUNK = 1024


def _tail_kernel(*refs, has_attn, final):
    it = iter(refs)
    x_ref = next(it)
    if has_attn:
        oin_ref, fwo_ref = next(it), next(it)
    gmem_ref, wq_ref, k_ref, v_ref, wo_ref, gff_ref, w1_ref, w2_ref = (next(it) for _ in range(8))
    if final:
        gfin_ref = next(it)
    out_ref = next(it)

    d = x_ref.shape[2]
    dh = d // MEM_HEADS
    x = x_ref[0]
    if has_attn:
        x = x + jnp.dot(oin_ref[0], fwo_ref[0], preferred_element_type=F32)

    q = (_mm(_rms(x, gmem_ref[0]), wq_ref[0]) * (1.0 / math.sqrt(dh))).astype(BF16)
    heads = []
    for hd in range(MEM_HEADS):
        sl = slice(hd * dh, (hd + 1) * dh)
        s = _mm_nt(q[:, sl], k_ref[0, 0, :, sl])
        p = jnp.exp(s - jnp.max(s, axis=-1, keepdims=True))
        l = jnp.sum(p, axis=-1, keepdims=True)
        heads.append(_mm(p, v_ref[0, 0, :, sl]) / l)
    x = x + _mm(jnp.concatenate(heads, axis=-1), wo_ref[0])

    hb = _rms(x, gff_ref[0]).astype(BF16)
    acc = jnp.zeros_like(x)
    for c in range(w1_ref.shape[2] // FF_CHUNK):
        sl = slice(c * FF_CHUNK, (c + 1) * FF_CHUNK)
        a = jnp.maximum(jnp.dot(hb, w1_ref[0, :, sl], preferred_element_type=F32), 0.0)
        acc = acc + _mm(a * a, w2_ref[0, sl, :])
    x = x + acc
    if final:
        x = _rms(x, gfin_ref[...])
    out_ref[0] = x


def _tail(x, layer, gmem, wq, k, v, wo, gff, w1, w2, attn=None, gfin=None):
    B, S, D = x.shape
    L = wq.shape[0]
    M = k.shape[2]
    F = w1.shape[2]
    lay = lambda b, s: (layer, 0, 0)
    tile = pl.BlockSpec((1, TM, D), lambda b, s: (b, s, 0))
    args, specs = [x], [tile]
    if attn is not None:
        o_in, fwo, fl = attn
        args += [o_in, fwo]
        specs += [pl.BlockSpec((1, TM, D), lambda b, s: (b, s, 0)),
                  _resident((1, D, D), lambda b, s: (fl, 0, 0))]
    args += [gmem.reshape(L, 1, D), wq, k, v, wo, gff.reshape(L, 1, D), w1, w2]
    specs += [
        _resident((1, 1, D), lay),
        _resident((1, D, D), lay),
        pl.BlockSpec((1, 1, M, D), lambda b, s: (layer, b, 0, 0)),
        pl.BlockSpec((1, 1, M, D), lambda b, s: (layer, b, 0, 0)),
        _resident((1, D, D), lay),
        _resident((1, 1, D), lay),
        _resident((1, D, F), lay),
        _resident((1, F, D), lay),
    ]
    if gfin is not None:
        args.append(gfin.reshape(1, D))
        specs.append(_resident((1, D), lambda b, s: (0, 0)))
    return pl.pallas_call(
        functools.partial(_tail_kernel, has_attn=attn is not None, final=gfin is not None),
        grid=(B, S // TM),
        in_specs=specs,
        out_specs=tile,
        out_shape=jax.ShapeDtypeStruct((B, S, D), F32),
        compiler_params=pltpu.CompilerParams(
            dimension_semantics=("arbitrary", "arbitrary"), vmem_limit_bytes=VMEM_LIMIT),
        name=f"tail{layer}",
    )(*args)


VT_ROWS = 80
C_PIECES = 3


def _bf16_pieces(x):
    pieces, rest = [], x
    for _ in range(C_PIECES):
        piece = rest.astype(BF16)
        pieces.append(piece)
        rest = rest - piece.astype(F32)
    return jnp.concatenate(pieces, axis=1)


def _kvfq_kernel(x_ref, gkv_ref, gq_ref, wk_ref, wvt_ref, ones_ref, wf_ref, fb_ref, place_ref, wq_ref,
                 k_ref, kc_ref, vt_ref, q_ref, crow_ref, carry):
    tm = x_ref.shape[1]
    dh = x_ref.shape[2] // FOX_HEADS

    @pl.when(pl.program_id(1) == 0)
    def _():
        carry[...] = jnp.zeros_like(carry)

    x = x_ref[0]
    hk = _rms(x, gkv_ref[...]).astype(BF16)
    k_ref[0] = jnp.dot(hk, wk_ref[...], preferred_element_type=F32).astype(BF16)
    vt_ref[0] = (_mm_nt(wvt_ref[...], hk) + jnp.tile(ones_ref[...], (1, tm // LANES))).astype(BF16)

    f = jnp.dot(hk, wf_ref[...], preferred_element_type=F32) + fb_ref[...]
    log_f = jnp.minimum(f, 0.0) - jnp.log(1.0 + jnp.exp(-jnp.abs(f)))
    tri = (lax.broadcasted_iota(jnp.int32, (tm, tm), 1)
           <= lax.broadcasted_iota(jnp.int32, (tm, tm), 0)).astype(BF16)
    parts = jnp.dot(tri, _bf16_pieces(log_f), preferred_element_type=F32)
    c = sum(parts[:, i * LANES:(i + 1) * LANES] for i in range(C_PIECES)) + carry[0:1, :]
    carry[0:1, :] = c[tm - 1:tm, :]
    c2 = c * LOG2E
    crow_ref[0] = c2.T[0:FOX_HEADS, :]

    kc_ref[0] = jnp.dot(_bf16_pieces(c2), place_ref[...], preferred_element_type=F32).astype(BF16)

    hq = _rms(x, gq_ref[...]).astype(BF16)
    q_ref[0] = (_mm_nt(wq_ref[0], hq) * (LOG2E / math.sqrt(dh))).astype(BF16)


def _gate_placement(n_heads):
    place = np.zeros((C_PIECES * LANES, n_heads // 2 * LANES), np.float32)
    for h in range(n_heads):
        for i in range(C_PIECES):
            place[i * LANES + h, (h // 2) * LANES + (h % 2) * C_PIECES + i] = 1.0
    return jnp.asarray(place, BF16)


def _kvfq(x, gkv, gq, wk, wvt, vt_ones, wf, fb, wq, layer):
    place = _gate_placement(FOX_HEADS)
    B, S, D = x.shape
    R = wvt.shape[0]
    const = lambda b, s: (0, 0)
    tile = pl.BlockSpec((1, TM, D), lambda b, s: (b, s, 0))
    return pl.pallas_call(
        _kvfq_kernel,
        grid=(B, S // TM),
        in_specs=[
            tile,
            _resident((1, D), const),
            _resident((1, D), const),
            _resident((D, D), const),
            _resident((R, D), const),
            _resident((R, LANES), const),
            _resident((D, LANES), const),
            _resident((1, LANES), const),
            _resident(place.shape, const),
            _resident((1, D, D), lambda b, s: (layer, 0, 0)),
        ],
        out_specs=[
            tile,
            tile,
            pl.BlockSpec((1, R, TM), lambda b, s: (b, 0, s)),
            pl.BlockSpec((1, D, TM), lambda b, s: (b, 0, s)),
            pl.BlockSpec((1, FOX_HEADS, TM), lambda b, s: (b, 0, s)),
        ],
        out_shape=[
            jax.ShapeDtypeStruct((B, S, D), BF16),
            jax.ShapeDtypeStruct((B, S, D), BF16),
            jax.ShapeDtypeStruct((B, R, S), BF16),
            jax.ShapeDtypeStruct((B, D, S), BF16),
            jax.ShapeDtypeStruct((B, FOX_HEADS, S), F32),
        ],
        scratch_shapes=[pltpu.VMEM((8, LANES), F32)],
        compiler_params=pltpu.CompilerParams(
            dimension_semantics=("arbitrary", "arbitrary"), vmem_limit_bytes=VMEM_LIMIT),
        name="kvfq",
    )(x, gkv.reshape(1, D), gq.reshape(1, D), wk, wvt, vt_ones, wf, fb, place, wq)


FOX_GROUP = 8
FOX_LAG = 2


def _fox_kernel(q_ref, k_ref, kc_ref, vt_ref, crow_ref, o_ref, tbuf, acc):
    tq = q_ref.shape[2]
    dh = LANES // 2
    grp = pl.program_id(1)
    qi = pl.program_id(2)
    G = FOX_GROUP

    tk = tbuf.shape[1]
    n_sub = tq // tk
    row = lax.broadcasted_iota(jnp.int32, (LANES, tq), 0)
    key_minus_query = (lax.broadcasted_iota(jnp.int32, (tk, tq), 0) - lax.broadcasted_iota(jnp.int32, (tk, tq), 1))
    qx, cqs = [], []
    for h in range(G):
        pr, hh = divmod(h, 2)
        qp = q_ref[0, pr * LANES:(pr + 1) * LANES, :]
        gate = (row >= hh * C_PIECES) & (row < (hh + 1) * C_PIECES)
        qx.append(jnp.concatenate([jnp.where(row // dh == hh, qp, jnp.zeros_like(qp)),
                                   jnp.where(gate, -1.0, 0.0).astype(BF16)], axis=0))
        cqs.append(crow_ref[0, pl.ds(G * grp + h, 1), :])
        acc[h] = jnp.zeros(acc.shape[1:], F32)

    def scores(kb, h, diag_offset):
        pr = h // 2
        k0 = pl.multiple_of(kb * tk, tk)
        kx = jnp.concatenate([k_ref[0, pl.ds(k0, tk), pr * LANES:(pr + 1) * LANES],
                              kc_ref[0, pl.ds(k0, tk), pr * LANES:(pr + 1) * LANES]], axis=1)
        t = jnp.dot(kx, qx[h], preferred_element_type=F32)
        if diag_offset is not None:
            t = jnp.where(key_minus_query <= -diag_offset, t, -jnp.inf)
        tbuf[h] = t
        return jnp.max(t, axis=0, keepdims=True)

    def consume(kb, h, mb, m):
        k0 = pl.multiple_of(kb * tk, tk)
        m_new = jnp.maximum(m, mb + cqs[h])
        p = jnp.exp2(tbuf[h] + (cqs[h] - m_new)).astype(BF16)
        v = vt_ref[0, h * VT_ROWS:(h + 1) * VT_ROWS, pl.ds(k0, tk)]
        acc[h] = jnp.exp2(m - m_new) * acc[h] + jnp.dot(v, p, preferred_element_type=F32)
        return m_new

    def sweep(kb, state, diag_offset=None):
        kb_prev, pending, ms = state
        ms, mbs = list(ms), []
        for h in range(G):
            mbs.append(scores(kb, h, diag_offset))
            if h >= FOX_LAG:
                ms[h - FOX_LAG] = consume(kb, h - FOX_LAG, mbs[h - FOX_LAG], ms[h - FOX_LAG])
            elif pending is not None:
                hp = h + G - FOX_LAG
                ms[hp] = consume(kb_prev, hp, pending[h], ms[hp])
        return kb, tuple(mbs[G - FOX_LAG:]), tuple(ms)

    def step(j, state):
        for i in range(n_sub):
            state = sweep(n_sub * j + i, state)
        return state

    state = (None, None, (jnp.full((1, tq), -jnp.inf, F32),) * G)
    for i in range(n_sub):
        state = sweep(n_sub * qi + i, state, diag_offset=i * tk)
    kb_prev, pending, ms = lax.fori_loop(0, qi, step, state)
    for h in range(FOX_LAG):
        hp = h + G - FOX_LAG
        consume(kb_prev, hp, pending[h], ms[hp])
    o_ref[0] = jnp.concatenate([acc[h, 0:dh] / acc[h, dh:dh + 1] for h in range(G)], axis=0).T.astype(BF16)


def _fox_attn(qt, k, kc, vt, crow):
    B, S, D = k.shape
    G = FOX_GROUP
    W = G * (LANES // 2)
    return pl.pallas_call(
        _fox_kernel,
        grid=(B, D // W, S // TQ),
        in_specs=[
            pl.BlockSpec((1, W, TQ), lambda b, g, i: (b, g, i)),
            pl.BlockSpec((1, S, W), lambda b, g, i: (b, 0, g)),
            pl.BlockSpec((1, S, W), lambda b, g, i: (b, 0, g)),
            pl.BlockSpec((1, G * VT_ROWS, S), lambda b, g, i: (b, g, 0)),
            pl.BlockSpec((1, FOX_HEADS, TQ), lambda b, g, i: (b, 0, i)),
        ],
        out_specs=pl.BlockSpec((1, TQ, W), lambda b, g, i: (b, i, g)),
        out_shape=jax.ShapeDtypeStruct((B, S, D), BF16),
        scratch_shapes=[pltpu.VMEM((G, TK, TQ), F32), pltpu.VMEM((G, VT_ROWS, TQ), F32)],
        compiler_params=pltpu.CompilerParams(
            dimension_semantics=("arbitrary", "arbitrary", "arbitrary"), vmem_limit_bytes=VMEM_LIMIT),
        name="fox_attn",
    )(qt, k, kc, vt, crow)


def kernel(x, mem, norm_mix_g, norm_mem_g, norm_memsrc_g, norm_ff_g, mem_wq, mem_wk, mem_wv, mem_wo, ff_w1, ff_w2,
           conv_pw1_w, conv_pw1_b, conv_dw_w, conv_dw_b, conv_ln_g, conv_ln_b, conv_pw2_w, conv_pw2_b,
           kv_norm_g, kvf_w, fgate_b, fox_wq, fox_wo, final_norm_g):
    D = x.shape[2]
    depth = norm_mix_g.shape[0]
    n_conv = conv_pw1_w.shape[0]
    bf = lambda w: w.astype(BF16)
    mem_wq_b, mem_wk_b, mem_wv_b, mem_wo_b = bf(mem_wq), bf(mem_wk), bf(mem_wv), bf(mem_wo)
    ff_w1_b, ff_w2_b = bf(ff_w1), bf(ff_w2)
    pw1_b, pw2_b = bf(conv_pw1_w), bf(conv_pw2_w)
    fox_wqt_b, fox_wo_b = bf(fox_wq).transpose(0, 2, 1), bf(fox_wo)
    kvf_b = bf(kvf_w)
    wk_b = kvf_b[:, :D]
    dh = D // FOX_HEADS
    wvt_b = jnp.pad(kvf_b[:, D:2 * D].T.reshape(FOX_HEADS, dh, D),
                    ((0, 0), (0, VT_ROWS - dh), (0, 0))).reshape(FOX_HEADS * VT_ROWS, D)
    vt_ones = jnp.tile((jnp.arange(VT_ROWS) >= dh).astype(F32)[:, None], (FOX_HEADS, LANES))
    wf_b = jnp.pad(kvf_b[:, 2 * D:], ((0, 0), (0, LANES - FOX_HEADS)))
    fb = jnp.pad(fgate_b, (0, LANES - FOX_HEADS)).reshape(1, LANES)

    assert depth == 2 and n_conv == 1 and fox_wq.shape[0] == 1

    mk, mv = _mem_kv(mem, norm_memsrc_g, mem_wk_b, mem_wv_b)
    tail = functools.partial(_tail, gmem=norm_mem_g, wq=mem_wq_b, k=mk, v=mv, wo=mem_wo_b,
                             gff=norm_ff_g, w1=ff_w1_b, w2=ff_w2_b)

    x = _conv_mix(x, norm_mix_g[0], pw1_b[0], conv_pw1_b[0], conv_dw_w[0], conv_dw_b[0],
                  conv_ln_g[0], conv_ln_b[0], pw2_b[0], conv_pw2_b[0])
    x = tail(x, 0)
    k, kc, vt, qt, crow = _kvfq(x, kv_norm_g, norm_mix_g[1], wk_b, wvt_b, vt_ones, wf_b, fb, fox_wqt_b, 0)
    o = _fox_attn(qt, k, kc, vt, crow)
    return tail(x, 1, attn=(o, fox_wo_b, 0), gfin=final_norm_g)
```

```python
import functools
import math

import jax
import jax.numpy as jnp
import numpy as np
from jax import lax
from jax.experimental import pallas as pl
from jax.experimental.pallas import tpu as pltpu

F32 = jnp.float32
BF16 = jnp.bfloat16

CONV_WIDTH = 31
FOX_HEADS = 16
MEM_HEADS = 4
RMS_EPS = 1e-6
LN_EPS = 1e-5
LOG2E = 1.4426950408889634

LANES = 128
SUBLANES = 8
HALO = 32
TM = 512
TQ = 512
TK = 256
VMEM_LIMIT = 56 * 1024 * 1024


def _rms(x, g):
    return x * lax.rsqrt(jnp.mean(x * x, axis=-1, keepdims=True) + RMS_EPS) * g


def _mm(a, w):
    return jnp.dot(a.astype(BF16), w, preferred_element_type=F32)


def _mm_nt(a, b):
    return lax.dot_general(a, b, (((1,), (1,)), ((), ())), preferred_element_type=F32)


def _resident(shape, index_map):
    return pl.BlockSpec(shape, index_map, pipeline_mode=pl.Buffered(1))


def _mem_kv_kernel(mem_ref, g_ref, wk_ref, wv_ref, k_ref, v_ref):
    mn = _rms(mem_ref[0], g_ref[0]).astype(BF16)
    k_ref[0, 0] = jnp.dot(mn, wk_ref[0], preferred_element_type=F32).astype(BF16)
    v_ref[0, 0] = jnp.dot(mn, wv_ref[0], preferred_element_type=F32).astype(BF16)


def _mem_kv(mem, g, wk, wv):
    B, M, D = mem.shape
    L = wk.shape[0]
    out = jax.ShapeDtypeStruct((L, B, M, D), BF16)
    return pl.pallas_call(
        _mem_kv_kernel,
        grid=(L, B),
        in_specs=[
            pl.BlockSpec((1, M, D), lambda l, b: (b, 0, 0)),
            pl.BlockSpec((1, 1, D), lambda l, b: (l, 0, 0)),
            pl.BlockSpec((1, D, D), lambda l, b: (l, 0, 0)),
            pl.BlockSpec((1, D, D), lambda l, b: (l, 0, 0)),
        ],
        out_specs=[
            pl.BlockSpec((1, 1, M, D), lambda l, b: (l, b, 0, 0)),
            pl.BlockSpec((1, 1, M, D), lambda l, b: (l, b, 0, 0)),
        ],
        out_shape=[out, out],
        compiler_params=pltpu.CompilerParams(
            dimension_semantics=("arbitrary", "arbitrary"), vmem_limit_bytes=VMEM_LIMIT),
        name="mem_kv",
    )(mem, g.reshape(L, 1, D), wk, wv)


CONV_ROWS = 128
CONV_COLS = LANES


def _conv_mixer_stages(x, g_ref, w1_ref, b1_ref, wdw_ref, bdw_ref, lng_ref, lnb_ref, w2_ref, b2_ref, ubuf, cbuf):
    tm, d = x.shape
    u = _mm(_rms(x, g_ref[...]), w1_ref[...]) + b1_ref[...]
    ubuf[HALO:HALO + tm, :] = u[:, :d] * jax.nn.sigmoid(u[:, d:])
    yield None

    base = HALO - (CONV_WIDTH - 1)
    for r0 in range(0, tm, CONV_ROWS):
        for c0 in range(0, d, CONV_COLS):
            cols = slice(c0, c0 + CONV_COLS)
            win = ubuf[r0:r0 + CONV_ROWS + HALO, cols]
            acc = jnp.broadcast_to(bdw_ref[:, cols], (CONV_ROWS, CONV_COLS))
            for r in range(SUBLANES):
                taps = [o for o in range(base, base + CONV_WIDTH) if o % SUBLANES == r]
                sh = win if r == 0 else pltpu.roll(win, CONV_ROWS + HALO - r, axis=0)
                for o in taps:
                    acc = acc + wdw_ref[o - base:o - base + 1, cols] * sh[o - r:o - r + CONV_ROWS]
            cbuf[r0:r0 + CONV_ROWS, cols] = acc
            yield None
    ubuf[0:HALO, :] = ubuf[tm:tm + HALO, :]

    c = cbuf[...]
    mu = jnp.mean(c, axis=-1, keepdims=True)
    cc = c - mu
    var = jnp.mean(cc * cc, axis=-1, keepdims=True)
    y = cc * lax.rsqrt(var + LN_EPS) * lng_ref[...] + lnb_ref[...]
    y = y * jax.nn.sigmoid(y)
    yield x + _mm(y, w2_ref[...]) + b2_ref[...]


FF_CHUNK = 1024


def _mem_attn_mlp_stages(x, gmem_ref, wq_ref, k_ref, v_ref, wo_ref, gff_ref, w1_ref, w2_ref):
    dh = x.shape[1] // MEM_HEADS
    q = (_mm(_rms(x, gmem_ref[0]), wq_ref[0]) * (1.0 / math.sqrt(dh))).astype(BF16)
    yield None
    heads = []
    for hd in range(MEM_HEADS):
        sl = slice(hd * dh, (hd + 1) * dh)
        s = _mm_nt(q[:, sl], k_ref[0, 0, :, sl])
        p = jnp.exp(s - jnp.max(s, axis=-1, keepdims=True))
        l = jnp.sum(p, axis=-1, keepdims=True)
        heads.append(_mm(p, v_ref[0, 0, :, sl]) / l)
    yield None
    x = x + _mm(jnp.concatenate(heads, axis=-1), wo_ref[0])
    yield None

    hb = _rms(x, gff_ref[0]).astype(BF16)
    acc = jnp.zeros_like(x)
    for c in range(w1_ref.shape[2] // FF_CHUNK):
        sl = slice(c * FF_CHUNK, (c + 1) * FF_CHUNK)
        a = jnp.maximum(jnp.dot(hb, w1_ref[0, :, sl], preferred_element_type=F32), 0.0)
        yield None
        acc = acc + _mm(a * a, w2_ref[0, sl, :])
        yield None
    yield x + acc


def _run(stages):
    for result in stages:
        pass
    return result


def _interleave(a_stages, b_stages, b_per_a):
    ra = rb = None
    a_live = b_live = True
    while a_live or b_live:
        if a_live:
            try:
                ra = next(a_stages)
            except StopIteration:
                a_live = False
        n_b = 0
        while b_live and (n_b < b_per_a or not a_live):
            try:
                rb = next(b_stages)
                n_b += 1
            except StopIteration:
                b_live = False
    return ra, rb


def _tail_kernel(*refs, has_attn, final):
    it = iter(refs)
    x_ref = next(it)
    if has_attn:
        oin_ref, fwo_ref = next(it), next(it)
    tail_refs = [next(it) for _ in range(8)]
    if final:
        gfin_ref = next(it)
    out_ref = next(it)

    x = x_ref[0]
    if has_attn:
        x = x + jnp.dot(oin_ref[0], fwo_ref[0], preferred_element_type=F32)
    x = _run(_mem_attn_mlp_stages(x, *tail_refs))
    if final:
        x = _rms(x, gfin_ref[...])
    out_ref[0] = x


MIXER_STAGES_PER_TAIL_STAGE = 3


def _mix_tail_kernel(x_ref, *refs, tiles_per_seq):
    conv_refs, tail_refs = refs[:9], refs[9:17]
    out_ref, ubuf, cbuf, x1buf = refs[17:]
    t = pl.program_id(0)

    @pl.when(t == 0)
    def _():
        x1buf[...] = jnp.zeros_like(x1buf)

    @pl.when(t % tiles_per_seq == 0)
    def _():
        ubuf[0:HALO, :] = jnp.zeros((HALO, ubuf.shape[1]), F32)

    out, x1_next = _interleave(_mem_attn_mlp_stages(x1buf[...], *tail_refs),
                               _conv_mixer_stages(x_ref[0], *conv_refs, ubuf, cbuf), MIXER_STAGES_PER_TAIL_STAGE)
    out_ref[0] = out
    x1buf[...] = x1_next


def _mix_tail(x, conv_params, layer, gmem, wq, k, v, wo, gff, w1, w2):
    g, cw1, cb1, wdw, bdw, lng, lnb, cw2, cb2 = conv_params
    B, S, D = x.shape
    L = wq.shape[0]
    M = k.shape[2]
    F = w1.shape[2]
    n_seq = S // TM
    n_tiles = B * n_seq
    row = lambda v: v.reshape(1, -1)
    const = lambda t: (0, 0)
    lay = lambda t: (layer, 0, 0)

    def mixer_tile(t):
        t = jnp.minimum(t, n_tiles - 1)
        return t // n_seq, t % n_seq, 0

    def tail_tile(t):
        t = jnp.maximum(t - 1, 0)
        return t // n_seq, t % n_seq, 0

    mem_block = pl.BlockSpec((1, 1, M, D), lambda t: (layer, jnp.maximum(t - 1, 0) // n_seq, 0, 0))
    return pl.pallas_call(
        functools.partial(_mix_tail_kernel, tiles_per_seq=n_seq),
        grid=(n_tiles + 1,),
        in_specs=[
            pl.BlockSpec((1, TM, D), mixer_tile),
            _resident((1, D), const),
            _resident((D, 2 * D), const),
            _resident((1, 2 * D), const),
            _resident((CONV_WIDTH, D), const),
            _resident((1, D), const),
            _resident((1, D), const),
            _resident((1, D), const),
            _resident((D, D), const),
            _resident((1, D), const),
            _resident((1, 1, D), lay),
            _resident((1, D, D), lay),
            mem_block,
            mem_block,
            _resident((1, D, D), lay),
            _resident((1, 1, D), lay),
            _resident((1, D, F), lay),
            _resident((1, F, D), lay),
        ],
        out_specs=pl.BlockSpec((1, TM, D), tail_tile),
        out_shape=jax.ShapeDtypeStruct((B, S, D), F32),
        scratch_shapes=[pltpu.VMEM((TM + HALO, D), F32), pltpu.VMEM((TM, D), F32), pltpu.VMEM((TM, D), F32)],
        compiler_params=pltpu.CompilerParams(
            dimension_semantics=("arbitrary",), vmem_limit_bytes=VMEM_LIMIT),
        name="mix_tail",
    )(x, row(g), cw1, row(cb1), wdw, row(bdw), row(lng), row(lnb), cw2, row(cb2),
      gmem.reshape(L, 1, D), wq, k, v, wo, gff.reshape(L, 1, D), w1, w2)


def _tail(x, layer, gmem, wq, k, v, wo, gff, w1, w2, attn=None, gfin=None):
    B, S, D = x.shape
    L = wq.shape[0]
    M = k.shape[2]
    F = w1.shape[2]
    lay = lambda b, s: (layer, 0, 0)
    tile = pl.BlockSpec((1, TM, D), lambda b, s: (b, s, 0))
    args, specs = [x], [tile]
    if attn is not None:
        o_in, fwo, fl = attn
        args += [o_in, fwo]
        specs += [pl.BlockSpec((1, TM, D), lambda b, s: (b, s, 0)),
                  _resident((1, D, D), lambda b, s: (fl, 0, 0))]
    args += [gmem.reshape(L, 1, D), wq, k, v, wo, gff.reshape(L, 1, D), w1, w2]
    specs += [
        _resident((1, 1, D), lay),
        _resident((1, D, D), lay),
        pl.BlockSpec((1, 1, M, D), lambda b, s: (layer, b, 0, 0)),
        pl.BlockSpec((1, 1, M, D), lambda b, s: (layer, b, 0, 0)),
        _resident((1, D, D), lay),
        _resident((1, 1, D), lay),
        _resident((1, D, F), lay),
        _resident((1, F, D), lay),
    ]
    if gfin is not None:
        args.append(gfin.reshape(1, D))
        specs.append(_resident((1, D), lambda b, s: (0, 0)))
    return pl.pallas_call(
        functools.partial(_tail_kernel, has_attn=attn is not None, final=gfin is not None),
        grid=(B, S // TM),
        in_specs=specs,
        out_specs=tile,
        out_shape=jax.ShapeDtypeStruct((B, S, D), F32),
        compiler_params=pltpu.CompilerParams(
            dimension_semantics=("arbitrary", "arbitrary"), vmem_limit_bytes=VMEM_LIMIT),
        name=f"tail{layer}",
    )(*args)


VT_ROWS = 80
C_PIECES = 3


def _bf16_pieces(x):
    pieces, rest = [], x
    for _ in range(C_PIECES):
        piece = rest.astype(BF16)
        pieces.append(piece)
        rest = rest - piece.astype(F32)
    return jnp.concatenate(pieces, axis=1)


def _kvfq_kernel(x_ref, gkv_ref, gq_ref, wk_ref, wvt_ref, ones_ref, wf_ref, fb_ref, place_ref, wq_ref,
                 k_ref, kc_ref, vt_ref, q_ref, crow_ref, carry):
    tm = x_ref.shape[1]
    dh = x_ref.shape[2] // FOX_HEADS

    @pl.when(pl.program_id(1) == 0)
    def _():
        carry[...] = jnp.zeros_like(carry)

    x = x_ref[0]
    hk = _rms(x, gkv_ref[...]).astype(BF16)
    k_ref[0] = jnp.dot(hk, wk_ref[...], preferred_element_type=F32).astype(BF16)
    vt_ref[0] = (_mm_nt(wvt_ref[...], hk) + jnp.tile(ones_ref[...], (1, tm // LANES))).astype(BF16)

    f = jnp.dot(hk, wf_ref[...], preferred_element_type=F32) + fb_ref[...]
    log_f = jnp.minimum(f, 0.0) - jnp.log(1.0 + jnp.exp(-jnp.abs(f)))
    tri = (lax.broadcasted_iota(jnp.int32, (tm, tm), 1)
           <= lax.broadcasted_iota(jnp.int32, (tm, tm), 0)).astype(BF16)
    parts = jnp.dot(tri, _bf16_pieces(log_f), preferred_element_type=F32)
    c = sum(parts[:, i * LANES:(i + 1) * LANES] for i in range(C_PIECES)) + carry[0:1, :]
    carry[0:1, :] = c[tm - 1:tm, :]
    c2 = c * LOG2E
    crow_ref[0] = c2.T[0:FOX_HEADS, :]

    kc_ref[0] = jnp.dot(_bf16_pieces(c2), place_ref[...], preferred_element_type=F32).astype(BF16)

    hq = _rms(x, gq_ref[...]).astype(BF16)
    q_ref[0] = (_mm_nt(wq_ref[0], hq) * (LOG2E / math.sqrt(dh))).astype(BF16)


def _gate_placement(n_heads):
    place = np.zeros((C_PIECES * LANES, n_heads // 2 * LANES), np.float32)
    for h in range(n_heads):
        for i in range(C_PIECES):
            place[i * LANES + h, (h // 2) * LANES + (h % 2) * C_PIECES + i] = 1.0
    return jnp.asarray(place, BF16)


def _kvfq(x, gkv, gq, wk, wvt, vt_ones, wf, fb, wq, layer):
    place = _gate_placement(FOX_HEADS)
    B, S, D = x.shape
    R = wvt.shape[0]
    const = lambda b, s: (0, 0)
    tile = pl.BlockSpec((1, TM, D), lambda b, s: (b, s, 0))
    return pl.pallas_call(
        _kvfq_kernel,
        grid=(B, S // TM),
        in_specs=[
            tile,
            _resident((1, D), const),
            _resident((1, D), const),
            _resident((D, D), const),
            _resident((R, D), const),
            _resident((R, LANES), const),
            _resident((D, LANES), const),
            _resident((1, LANES), const),
            _resident(place.shape, const),
            _resident((1, D, D), lambda b, s: (layer, 0, 0)),
        ],
        out_specs=[
            tile,
            tile,
            pl.BlockSpec((1, R, TM), lambda b, s: (b, 0, s)),
            pl.BlockSpec((1, D, TM), lambda b, s: (b, 0, s)),
            pl.BlockSpec((1, FOX_HEADS, TM), lambda b, s: (b, 0, s)),
        ],
        out_shape=[
            jax.ShapeDtypeStruct((B, S, D), BF16),
            jax.ShapeDtypeStruct((B, S, D), BF16),
            jax.ShapeDtypeStruct((B, R, S), BF16),
            jax.ShapeDtypeStruct((B, D, S), BF16),
            jax.ShapeDtypeStruct((B, FOX_HEADS, S), F32),
        ],
        scratch_shapes=[pltpu.VMEM((8, LANES), F32)],
        compiler_params=pltpu.CompilerParams(
            dimension_semantics=("arbitrary", "arbitrary"), vmem_limit_bytes=VMEM_LIMIT),
        name="kvfq",
    )(x, gkv.reshape(1, D), gq.reshape(1, D), wk, wvt, vt_ones, wf, fb, place, wq)


FOX_GROUP = 8
FOX_LAG = 2


def _fox_kernel(q_ref, k_ref, kc_ref, vt_ref, crow_ref, o_ref, tbuf, acc):
    tq = q_ref.shape[2]
    dh = LANES // 2
    grp = pl.program_id(1)
    qi = pl.program_id(2)
    G = FOX_GROUP

    tk = tbuf.shape[1]
    n_sub = tq // tk
    row = lax.broadcasted_iota(jnp.int32, (LANES, tq), 0)
    key_minus_query = (lax.broadcasted_iota(jnp.int32, (tk, tq), 0) - lax.broadcasted_iota(jnp.int32, (tk, tq), 1))
    qx, cqs = [], []
    for h in range(G):
        pr, hh = divmod(h, 2)
        qp = q_ref[0, pr * LANES:(pr + 1) * LANES, :]
        gate = (row >= hh * C_PIECES) & (row < (hh + 1) * C_PIECES)
        qx.append(jnp.concatenate([jnp.where(row // dh == hh, qp, jnp.zeros_like(qp)),
                                   jnp.where(gate, -1.0, 0.0).astype(BF16)], axis=0))
        cqs.append(crow_ref[0, pl.ds(G * grp + h, 1), :])
        acc[h] = jnp.zeros(acc.shape[1:], F32)

    def scores(kb, h, diag_offset):
        pr = h // 2
        k0 = pl.multiple_of(kb * tk, tk)
        kx = jnp.concatenate([k_ref[0, pl.ds(k0, tk), pr * LANES:(pr + 1) * LANES],
                              kc_ref[0, pl.ds(k0, tk), pr * LANES:(pr + 1) * LANES]], axis=1)
        t = jnp.dot(kx, qx[h], preferred_element_type=F32)
        if diag_offset is not None:
            t = jnp.where(key_minus_query <= -diag_offset, t, -jnp.inf)
        tbuf[h] = t
        return jnp.max(t, axis=0, keepdims=True)

    def consume(kb, h, mb, m):
        k0 = pl.multiple_of(kb * tk, tk)
        m_new = jnp.maximum(m, mb + cqs[h])
        p = jnp.exp2(tbuf[h] + (cqs[h] - m_new)).astype(BF16)
        v = vt_ref[0, h * VT_ROWS:(h + 1) * VT_ROWS, pl.ds(k0, tk)]
        acc[h] = jnp.exp2(m - m_new) * acc[h] + jnp.dot(v, p, preferred_element_type=F32)
        return m_new

    def sweep(kb, state, diag_offset=None):
        kb_prev, pending, ms = state
        ms, mbs = list(ms), []
        for h in range(G):
            mbs.append(scores(kb, h, diag_offset))
            if h >= FOX_LAG:
                ms[h - FOX_LAG] = consume(kb, h - FOX_LAG, mbs[h - FOX_LAG], ms[h - FOX_LAG])
            elif pending is not None:
                hp = h + G - FOX_LAG
                ms[hp] = consume(kb_prev, hp, pending[h], ms[hp])
        return kb, tuple(mbs[G - FOX_LAG:]), tuple(ms)

    def step(j, state):
        for i in range(n_sub):
            state = sweep(n_sub * j + i, state)
        return state

    state = (None, None, (jnp.full((1, tq), -jnp.inf, F32),) * G)
    for i in range(n_sub):
        state = sweep(n_sub * qi + i, state, diag_offset=i * tk)
    kb_prev, pending, ms = lax.fori_loop(0, qi, step, state)
    for h in range(FOX_LAG):
        hp = h + G - FOX_LAG
        consume(kb_prev, hp, pending[h], ms[hp])
    o_ref[0] = jnp.concatenate([acc[h, 0:dh] / acc[h, dh:dh + 1] for h in range(G)], axis=0).T.astype(BF16)


def _fox_attn(qt, k, kc, vt, crow):
    B, S, D = k.shape
    G = FOX_GROUP
    W = G * (LANES // 2)
    return pl.pallas_call(
        _fox_kernel,
        grid=(B, D // W, S // TQ),
        in_specs=[
            pl.BlockSpec((1, W, TQ), lambda b, g, i: (b, g, i)),
            pl.BlockSpec((1, S, W), lambda b, g, i: (b, 0, g)),
            pl.BlockSpec((1, S, W), lambda b, g, i: (b, 0, g)),
            pl.BlockSpec((1, G * VT_ROWS, S), lambda b, g, i: (b, g, 0)),
            pl.BlockSpec((1, FOX_HEADS, TQ), lambda b, g, i: (b, 0, i)),
        ],
        out_specs=pl.BlockSpec((1, TQ, W), lambda b, g, i: (b, i, g)),
        out_shape=jax.ShapeDtypeStruct((B, S, D), BF16),
        scratch_shapes=[pltpu.VMEM((G, TK, TQ), F32), pltpu.VMEM((G, VT_ROWS, TQ), F32)],
        compiler_params=pltpu.CompilerParams(
            dimension_semantics=("arbitrary", "arbitrary", "arbitrary"), vmem_limit_bytes=VMEM_LIMIT),
        name="fox_attn",
    )(qt, k, kc, vt, crow)


def kernel(x, mem, norm_mix_g, norm_mem_g, norm_memsrc_g, norm_ff_g, mem_wq, mem_wk, mem_wv, mem_wo, ff_w1, ff_w2,
           conv_pw1_w, conv_pw1_b, conv_dw_w, conv_dw_b, conv_ln_g, conv_ln_b, conv_pw2_w, conv_pw2_b,
           kv_norm_g, kvf_w, fgate_b, fox_wq, fox_wo, final_norm_g):
    D = x.shape[2]
    depth = norm_mix_g.shape[0]
    n_conv = conv_pw1_w.shape[0]
    bf = lambda w: w.astype(BF16)
    mem_wq_b, mem_wk_b, mem_wv_b, mem_wo_b = bf(mem_wq), bf(mem_wk), bf(mem_wv), bf(mem_wo)
    ff_w1_b, ff_w2_b = bf(ff_w1), bf(ff_w2)
    pw1_b, pw2_b = bf(conv_pw1_w), bf(conv_pw2_w)
    fox_wqt_b, fox_wo_b = bf(fox_wq).transpose(0, 2, 1), bf(fox_wo)
    kvf_b = bf(kvf_w)
    wk_b = kvf_b[:, :D]
    dh = D // FOX_HEADS
    wvt_b = jnp.pad(kvf_b[:, D:2 * D].T.reshape(FOX_HEADS, dh, D),
                    ((0, 0), (0, VT_ROWS - dh), (0, 0))).reshape(FOX_HEADS * VT_ROWS, D)
    vt_ones = jnp.tile((jnp.arange(VT_ROWS) >= dh).astype(F32)[:, None], (FOX_HEADS, LANES))
    wf_b = jnp.pad(kvf_b[:, 2 * D:], ((0, 0), (0, LANES - FOX_HEADS)))
    fb = jnp.pad(fgate_b, (0, LANES - FOX_HEADS)).reshape(1, LANES)

    assert depth == 2 and n_conv == 1 and fox_wq.shape[0] == 1

    mk, mv = _mem_kv(mem, norm_memsrc_g, mem_wk_b, mem_wv_b)
    tail = functools.partial(_tail, gmem=norm_mem_g, wq=mem_wq_b, k=mk, v=mv, wo=mem_wo_b,
                             gff=norm_ff_g, w1=ff_w1_b, w2=ff_w2_b)

    conv_params = (norm_mix_g[0], pw1_b[0], conv_pw1_b[0], conv_dw_w[0], conv_dw_b[0],
                   conv_ln_g[0], conv_ln_b[0], pw2_b[0], conv_pw2_b[0])
    x = _mix_tail(x, conv_params, 0, norm_mem_g, mem_wq_b, mk, mv, mem_wo_b, norm_ff_g, ff_w1_b, ff_w2_b)
    k, kc, vt, qt, crow = _kvfq(x, kv_norm_g, norm_mix_g[1], wk_b, wvt_b, vt_ones, wf_b, fb, fox_wqt_b, 0)
    o = _fox_attn(qt, k, kc, vt, crow)
    return tail(x, 1, attn=(o, fox_wo_b, 0), gfin=final_norm_g)
```

```python
import functools
import math

import jax
import jax.numpy as jnp
import numpy as np
from jax import lax
from jax.experimental import pallas as pl
from jax.experimental.pallas import tpu as pltpu

F32 = jnp.float32
BF16 = jnp.bfloat16

CONV_WIDTH = 31
FOX_HEADS = 16
MEM_HEADS = 4
RMS_EPS = 1e-6
LN_EPS = 1e-5
LOG2E = 1.4426950408889634

LANES = 128
SUBLANES = 8
HALO = 32
TM = 512
TQ = 512
TK = 256
VMEM_LIMIT = 56 * 1024 * 1024


def _rms(x, g):
    return x * lax.rsqrt(jnp.mean(x * x, axis=-1, keepdims=True) + RMS_EPS) * g


def _mm(a, w):
    return jnp.dot(a.astype(BF16), w, preferred_element_type=F32)


def _mm_nt(a, b):
    return lax.dot_general(a, b, (((1,), (1,)), ((), ())), preferred_element_type=F32)


def _resident(shape, index_map):
    return pl.BlockSpec(shape, index_map, pipeline_mode=pl.Buffered(1))


def _mem_kv_kernel(mem_ref, g_ref, wk_ref, wv_ref, k_ref, v_ref):
    mn = _rms(mem_ref[0], g_ref[0]).astype(BF16)
    k_ref[0, 0] = jnp.dot(mn, wk_ref[0], preferred_element_type=F32).astype(BF16)
    v_ref[0, 0] = jnp.dot(mn, wv_ref[0], preferred_element_type=F32).astype(BF16)


def _mem_kv(mem, g, wk, wv):
    B, M, D = mem.shape
    L = wk.shape[0]
    out = jax.ShapeDtypeStruct((L, B, M, D), BF16)
    return pl.pallas_call(
        _mem_kv_kernel,
        grid=(L, B),
        in_specs=[
            pl.BlockSpec((1, M, D), lambda l, b: (b, 0, 0)),
            pl.BlockSpec((1, 1, D), lambda l, b: (l, 0, 0)),
            pl.BlockSpec((1, D, D), lambda l, b: (l, 0, 0)),
            pl.BlockSpec((1, D, D), lambda l, b: (l, 0, 0)),
        ],
        out_specs=[
            pl.BlockSpec((1, 1, M, D), lambda l, b: (l, b, 0, 0)),
            pl.BlockSpec((1, 1, M, D), lambda l, b: (l, b, 0, 0)),
        ],
        out_shape=[out, out],
        compiler_params=pltpu.CompilerParams(
            dimension_semantics=("arbitrary", "arbitrary"), vmem_limit_bytes=VMEM_LIMIT),
        name="mem_kv",
    )(mem, g.reshape(L, 1, D), wk, wv)


CONV_ROWS = 128
CONV_COLS = LANES


def _conv_mixer_stages(x, g_ref, w1_ref, b1_ref, wdw_ref, bdw_ref, lng_ref, lnb_ref, w2_ref, b2_ref, ubuf, cbuf):
    tm, d = x.shape
    u = _mm(_rms(x, g_ref[...]), w1_ref[...]) + b1_ref[...]
    ubuf[HALO:HALO + tm, :] = u[:, :d] * jax.nn.sigmoid(u[:, d:])
    yield None

    base = HALO - (CONV_WIDTH - 1)
    for r0 in range(0, tm, CONV_ROWS):
        for c0 in range(0, d, CONV_COLS):
            cols = slice(c0, c0 + CONV_COLS)
            win = ubuf[r0:r0 + CONV_ROWS + HALO, cols]
            acc = jnp.broadcast_to(bdw_ref[:, cols], (CONV_ROWS, CONV_COLS))
            for r in range(SUBLANES):
                taps = [o for o in range(base, base + CONV_WIDTH) if o % SUBLANES == r]
                sh = win if r == 0 else pltpu.roll(win, CONV_ROWS + HALO - r, axis=0)
                for o in taps:
                    acc = acc + wdw_ref[o - base:o - base + 1, cols] * sh[o - r:o - r + CONV_ROWS]
            cbuf[r0:r0 + CONV_ROWS, cols] = acc
            yield None
    ubuf[0:HALO, :] = ubuf[tm:tm + HALO, :]

    c = cbuf[...]
    mu = jnp.mean(c, axis=-1, keepdims=True)
    cc = c - mu
    var = jnp.mean(cc * cc, axis=-1, keepdims=True)
    y = cc * lax.rsqrt(var + LN_EPS) * lng_ref[...] + lnb_ref[...]
    y = y * jax.nn.sigmoid(y)
    yield x + _mm(y, w2_ref[...]) + b2_ref[...]


FF_CHUNK = 1024


def _mem_attn_mlp_stages(x, gmem_ref, wq_ref, k_ref, v_ref, wo_ref, gff_ref, w1_ref, w2_ref):
    dh = x.shape[1] // MEM_HEADS
    q = (_mm(_rms(x, gmem_ref[0]), wq_ref[0]) * (1.0 / math.sqrt(dh))).astype(BF16)
    yield None
    heads = []
    for hd in range(MEM_HEADS):
        sl = slice(hd * dh, (hd + 1) * dh)
        s = _mm_nt(q[:, sl], k_ref[0, 0, :, sl])
        p = jnp.exp(s - jnp.max(s, axis=-1, keepdims=True))
        l = jnp.sum(p, axis=-1, keepdims=True)
        heads.append(_mm(p, v_ref[0, 0, :, sl]) / l)
    yield None
    x = x + _mm(jnp.concatenate(heads, axis=-1), wo_ref[0])
    yield None

    hb = _rms(x, gff_ref[0]).astype(BF16)
    acc = jnp.zeros_like(x)
    for c in range(w1_ref.shape[2] // FF_CHUNK):
        sl = slice(c * FF_CHUNK, (c + 1) * FF_CHUNK)
        a = jnp.maximum(jnp.dot(hb, w1_ref[0, :, sl], preferred_element_type=F32), 0.0)
        yield None
        acc = acc + _mm(a * a, w2_ref[0, sl, :])
        yield None
    yield x + acc


def _run(stages):
    for result in stages:
        pass
    return result


def _interleave(a_stages, b_stages, b_per_a):
    ra = rb = None
    a_live = b_live = True
    while a_live or b_live:
        if a_live:
            try:
                ra = next(a_stages)
            except StopIteration:
                a_live = False
        n_b = 0
        while b_live and (n_b < b_per_a or not a_live):
            try:
                rb = next(b_stages)
                n_b += 1
            except StopIteration:
                b_live = False
    return ra, rb


def _tail_kernel(*refs, has_attn, final):
    it = iter(refs)
    x_ref = next(it)
    if has_attn:
        oin_ref, fwo_ref = next(it), next(it)
    tail_refs = [next(it) for _ in range(8)]
    if final:
        gfin_ref = next(it)
    out_ref = next(it)

    x = x_ref[0]
    if has_attn:
        x = x + jnp.dot(oin_ref[0], fwo_ref[0], preferred_element_type=F32)
    x = _run(_mem_attn_mlp_stages(x, *tail_refs))
    if final:
        x = _rms(x, gfin_ref[...])
    out_ref[0] = x


MIXER_STAGES_PER_TAIL_STAGE = 3


def _mix_tail_kernel(x_ref, *refs, tiles_per_seq):
    conv_refs, tail_refs = refs[:9], refs[9:17]
    out_ref, ubuf, cbuf, x1buf = refs[17:]
    t = pl.program_id(0)

    @pl.when(t == 0)
    def _():
        x1buf[...] = jnp.zeros_like(x1buf)

    @pl.when(t % tiles_per_seq == 0)
    def _():
        ubuf[0:HALO, :] = jnp.zeros((HALO, ubuf.shape[1]), F32)

    out, x1_next = _interleave(_mem_attn_mlp_stages(x1buf[...], *tail_refs),
                               _conv_mixer_stages(x_ref[0], *conv_refs, ubuf, cbuf), MIXER_STAGES_PER_TAIL_STAGE)
    out_ref[0] = out
    x1buf[...] = x1_next


def _mix_tail(x, conv_params, layer, gmem, wq, k, v, wo, gff, w1, w2):
    g, cw1, cb1, wdw, bdw, lng, lnb, cw2, cb2 = conv_params
    B, S, D = x.shape
    L = gmem.shape[0]
    M = k.shape[2]
    F = w1.shape[2]
    n_seq = S // TM
    n_tiles = B * n_seq
    row = lambda v: v.reshape(1, -1)
    const = lambda t: (0, 0)
    lay = lambda t: (layer, 0, 0)
    first = lambda t: (0, 0, 0)

    def mixer_tile(t):
        t = jnp.minimum(t, n_tiles - 1)
        return t // n_seq, t % n_seq, 0

    def tail_tile(t):
        t = jnp.maximum(t - 1, 0)
        return t // n_seq, t % n_seq, 0

    mem_block = pl.BlockSpec((1, 1, M, D), lambda t: (layer, jnp.maximum(t - 1, 0) // n_seq, 0, 0))
    args = (x, row(g), cw1, row(cb1), wdw, row(bdw), row(lng), row(lnb), cw2, row(cb2),
            gmem.reshape(L, 1, D), wq, k, v, wo, gff.reshape(L, 1, D), w1, w2)
    weights = (cw1, cw2, wq, wo, w1, w2)
    return pl.pallas_call(
        functools.partial(_mix_tail_kernel, tiles_per_seq=n_seq),
        grid=(n_tiles + 1,),
        in_specs=[
            pl.BlockSpec((1, TM, D), mixer_tile),
            _resident((1, D), const),
            _resident((D, 2 * D), const),
            _resident((1, 2 * D), const),
            _resident((CONV_WIDTH, D), const),
            _resident((1, D), const),
            _resident((1, D), const),
            _resident((1, D), const),
            _resident((D, D), const),
            _resident((1, D), const),
            _resident((1, 1, D), lay),
            _resident((1, D, D), first),
            mem_block,
            mem_block,
            _resident((1, D, D), first),
            _resident((1, 1, D), lay),
            _resident((1, D, F), first),
            _resident((1, F, D), first),
        ],
        out_specs=pl.BlockSpec((1, TM, D), tail_tile),
        out_shape=jax.ShapeDtypeStruct((B, S, D), F32),
        scratch_shapes=[pltpu.VMEM((TM + HALO, D), F32), pltpu.VMEM((TM, D), F32), pltpu.VMEM((TM, D), F32)],
        compiler_params=pltpu.CompilerParams(
            dimension_semantics=("arbitrary",), vmem_limit_bytes=VMEM_LIMIT,
            allow_input_fusion=[any(a is w for w in weights) for a in args]),
        name="mix_tail",
    )(*args)


def _tail(x, layer, gmem, wq, k, v, wo, gff, w1, w2, attn=None, gfin=None):
    B, S, D = x.shape
    L = gmem.shape[0]
    M = k.shape[2]
    F = w1.shape[2]
    lay = lambda b, s: (layer, 0, 0)
    first = lambda b, s: (0, 0, 0)
    tile = pl.BlockSpec((1, TM, D), lambda b, s: (b, s, 0))
    args, specs = [x], [tile]
    if attn is not None:
        o_in, fwo = attn
        args += [o_in, fwo]
        specs += [pl.BlockSpec((1, TM, D), lambda b, s: (b, s, 0)), _resident((1, D, D), first)]
    args += [gmem.reshape(L, 1, D), wq, k, v, wo, gff.reshape(L, 1, D), w1, w2]
    specs += [
        _resident((1, 1, D), lay),
        _resident((1, D, D), first),
        pl.BlockSpec((1, 1, M, D), lambda b, s: (layer, b, 0, 0)),
        pl.BlockSpec((1, 1, M, D), lambda b, s: (layer, b, 0, 0)),
        _resident((1, D, D), first),
        _resident((1, 1, D), lay),
        _resident((1, D, F), first),
        _resident((1, F, D), first),
    ]
    if gfin is not None:
        args.append(gfin.reshape(1, D))
        specs.append(_resident((1, D), lambda b, s: (0, 0)))
    weights = (wq, wo, w1, w2) + (() if attn is None else (attn[1],))
    return pl.pallas_call(
        functools.partial(_tail_kernel, has_attn=attn is not None, final=gfin is not None),
        grid=(B, S // TM),
        in_specs=specs,
        out_specs=tile,
        out_shape=jax.ShapeDtypeStruct((B, S, D), F32),
        compiler_params=pltpu.CompilerParams(
            dimension_semantics=("arbitrary", "arbitrary"), vmem_limit_bytes=VMEM_LIMIT,
            allow_input_fusion=[any(a is w for w in weights) for a in args]),
        name=f"tail{layer}",
    )(*args)


VT_ROWS = 80
C_PIECES = 3


def _bf16_pieces(x):
    pieces, rest = [], x
    for _ in range(C_PIECES):
        piece = rest.astype(BF16)
        pieces.append(piece)
        rest = rest - piece.astype(F32)
    return jnp.concatenate(pieces, axis=1)


def _kvfq_kernel(x_ref, gkv_ref, gq_ref, wk_ref, wvt_ref, ones_ref, wf_ref, fb_ref, place_ref, wq_ref,
                 k_ref, kc_ref, vt_ref, q_ref, crow_ref, carry):
    tm = x_ref.shape[1]
    dh = x_ref.shape[2] // FOX_HEADS

    @pl.when(pl.program_id(1) == 0)
    def _():
        carry[...] = jnp.zeros_like(carry)

    x = x_ref[0]
    hk = _rms(x, gkv_ref[...]).astype(BF16)
    k_ref[0] = jnp.dot(hk, wk_ref[...], preferred_element_type=F32).astype(BF16)
    vt_ref[0] = (_mm_nt(wvt_ref[...], hk) + jnp.tile(ones_ref[...], (1, tm // LANES))).astype(BF16)

    f = jnp.dot(hk, wf_ref[...], preferred_element_type=F32) + fb_ref[...]
    log_f = jnp.minimum(f, 0.0) - jnp.log(1.0 + jnp.exp(-jnp.abs(f)))
    tri = (lax.broadcasted_iota(jnp.int32, (tm, tm), 1)
           <= lax.broadcasted_iota(jnp.int32, (tm, tm), 0)).astype(BF16)
    parts = jnp.dot(tri, _bf16_pieces(log_f), preferred_element_type=F32)
    c = sum(parts[:, i * LANES:(i + 1) * LANES] for i in range(C_PIECES)) + carry[0:1, :]
    carry[0:1, :] = c[tm - 1:tm, :]
    c2 = c * LOG2E
    crow_ref[0] = c2.T[0:FOX_HEADS, :]

    kc_ref[0] = jnp.dot(_bf16_pieces(c2), place_ref[...], preferred_element_type=F32).astype(BF16)

    hq = _rms(x, gq_ref[...]).astype(BF16)
    q_ref[0] = (_mm_nt(wq_ref[0], hq) * (LOG2E / math.sqrt(dh))).astype(BF16)


def _gate_placement(n_heads):
    place = np.zeros((C_PIECES * LANES, n_heads // 2 * LANES), np.float32)
    for h in range(n_heads):
        for i in range(C_PIECES):
            place[i * LANES + h, (h // 2) * LANES + (h % 2) * C_PIECES + i] = 1.0
    return jnp.asarray(place, BF16)


def _kvfq(x, gkv, gq, wk, wvt, vt_ones, wf, fb, wq, layer):
    place = _gate_placement(FOX_HEADS)
    B, S, D = x.shape
    R = wvt.shape[0]
    const = lambda b, s: (0, 0)
    tile = pl.BlockSpec((1, TM, D), lambda b, s: (b, s, 0))
    return pl.pallas_call(
        _kvfq_kernel,
        grid=(B, S // TM),
        in_specs=[
            tile,
            _resident((1, D), const),
            _resident((1, D), const),
            _resident((D, D), const),
            _resident((R, D), const),
            _resident((R, LANES), const),
            _resident((D, LANES), const),
            _resident((1, LANES), const),
            _resident(place.shape, const),
            _resident((1, D, D), lambda b, s: (layer, 0, 0)),
        ],
        out_specs=[
            tile,
            tile,
            pl.BlockSpec((1, R, TM), lambda b, s: (b, 0, s)),
            pl.BlockSpec((1, D, TM), lambda b, s: (b, 0, s)),
            pl.BlockSpec((1, FOX_HEADS, TM), lambda b, s: (b, 0, s)),
        ],
        out_shape=[
            jax.ShapeDtypeStruct((B, S, D), BF16),
            jax.ShapeDtypeStruct((B, S, D), BF16),
            jax.ShapeDtypeStruct((B, R, S), BF16),
            jax.ShapeDtypeStruct((B, D, S), BF16),
            jax.ShapeDtypeStruct((B, FOX_HEADS, S), F32),
        ],
        scratch_shapes=[pltpu.VMEM((8, LANES), F32)],
        compiler_params=pltpu.CompilerParams(
            dimension_semantics=("arbitrary", "arbitrary"), vmem_limit_bytes=VMEM_LIMIT),
        name="kvfq",
    )(x, gkv.reshape(1, D), gq.reshape(1, D), wk, wvt, vt_ones, wf, fb, place, wq)


FOX_GROUP = 8
FOX_LAG = 2


def _fox_kernel(q_ref, k_ref, kc_ref, vt_ref, crow_ref, o_ref, tbuf, acc):
    tq = q_ref.shape[2]
    dh = LANES // 2
    grp = pl.program_id(1)
    qi = pl.program_id(2)
    G = FOX_GROUP

    tk = tbuf.shape[1]
    n_sub = tq // tk
    row = lax.broadcasted_iota(jnp.int32, (LANES, tq), 0)
    key_minus_query = (lax.broadcasted_iota(jnp.int32, (tk, tq), 0) - lax.broadcasted_iota(jnp.int32, (tk, tq), 1))
    qx, cqs = [], []
    for h in range(G):
        pr, hh = divmod(h, 2)
        qp = q_ref[0, pr * LANES:(pr + 1) * LANES, :]
        gate = (row >= hh * C_PIECES) & (row < (hh + 1) * C_PIECES)
        qx.append(jnp.concatenate([jnp.where(row // dh == hh, qp, jnp.zeros_like(qp)),
                                   jnp.where(gate, -1.0, 0.0).astype(BF16)], axis=0))
        cqs.append(crow_ref[0, pl.ds(G * grp + h, 1), :])
        acc[h] = jnp.zeros(acc.shape[1:], F32)

    def scores(kb, h, diag_offset):
        pr = h // 2
        k0 = pl.multiple_of(kb * tk, tk)
        kx = jnp.concatenate([k_ref[0, pl.ds(k0, tk), pr * LANES:(pr + 1) * LANES],
                              kc_ref[0, pl.ds(k0, tk), pr * LANES:(pr + 1) * LANES]], axis=1)
        t = jnp.dot(kx, qx[h], preferred_element_type=F32)
        if diag_offset is not None:
            t = jnp.where(key_minus_query <= -diag_offset, t, -jnp.inf)
        tbuf[h] = t
        return jnp.max(t, axis=0, keepdims=True)

    def consume(kb, h, mb, m):
        k0 = pl.multiple_of(kb * tk, tk)
        m_new = jnp.maximum(m, mb + cqs[h])
        p = jnp.exp2(tbuf[h] + (cqs[h] - m_new)).astype(BF16)
        v = vt_ref[0, h * VT_ROWS:(h + 1) * VT_ROWS, pl.ds(k0, tk)]
        acc[h] = jnp.exp2(m - m_new) * acc[h] + jnp.dot(v, p, preferred_element_type=F32)
        return m_new

    def sweep(kb, state, diag_offset=None):
        kb_prev, pending, ms = state
        ms, mbs = list(ms), []
        for h in range(G):
            mbs.append(scores(kb, h, diag_offset))
            if h >= FOX_LAG:
                ms[h - FOX_LAG] = consume(kb, h - FOX_LAG, mbs[h - FOX_LAG], ms[h - FOX_LAG])
            elif pending is not None:
                hp = h + G - FOX_LAG
                ms[hp] = consume(kb_prev, hp, pending[h], ms[hp])
        return kb, tuple(mbs[G - FOX_LAG:]), tuple(ms)

    def step(j, state):
        for i in range(n_sub):
            state = sweep(n_sub * j + i, state)
        return state

    state = (None, None, (jnp.full((1, tq), -jnp.inf, F32),) * G)
    for i in range(n_sub):
        state = sweep(n_sub * qi + i, state, diag_offset=i * tk)
    kb_prev, pending, ms = lax.fori_loop(0, qi, step, state)
    for h in range(FOX_LAG):
        hp = h + G - FOX_LAG
        consume(kb_prev, hp, pending[h], ms[hp])
    o_ref[0] = jnp.concatenate([acc[h, 0:dh] / acc[h, dh:dh + 1] for h in range(G)], axis=0).T.astype(BF16)


def _fox_attn(qt, k, kc, vt, crow):
    B, S, D = k.shape
    G = FOX_GROUP
    W = G * (LANES // 2)
    return pl.pallas_call(
        _fox_kernel,
        grid=(B, D // W, S // TQ),
        in_specs=[
            pl.BlockSpec((1, W, TQ), lambda b, g, i: (b, g, i)),
            pl.BlockSpec((1, S, W), lambda b, g, i: (b, 0, g)),
            pl.BlockSpec((1, S, W), lambda b, g, i: (b, 0, g)),
            pl.BlockSpec((1, G * VT_ROWS, S), lambda b, g, i: (b, g, 0)),
            pl.BlockSpec((1, FOX_HEADS, TQ), lambda b, g, i: (b, 0, i)),
        ],
        out_specs=pl.BlockSpec((1, TQ, W), lambda b, g, i: (b, i, g)),
        out_shape=jax.ShapeDtypeStruct((B, S, D), BF16),
        scratch_shapes=[pltpu.VMEM((G, TK, TQ), F32), pltpu.VMEM((G, VT_ROWS, TQ), F32)],
        compiler_params=pltpu.CompilerParams(
            dimension_semantics=("arbitrary", "arbitrary", "arbitrary"), vmem_limit_bytes=VMEM_LIMIT),
        name="fox_attn",
    )(qt, k, kc, vt, crow)


def kernel(x, mem, norm_mix_g, norm_mem_g, norm_memsrc_g, norm_ff_g, mem_wq, mem_wk, mem_wv, mem_wo, ff_w1, ff_w2,
           conv_pw1_w, conv_pw1_b, conv_dw_w, conv_dw_b, conv_ln_g, conv_ln_b, conv_pw2_w, conv_pw2_b,
           kv_norm_g, kvf_w, fgate_b, fox_wq, fox_wo, final_norm_g):
    D = x.shape[2]
    depth = norm_mix_g.shape[0]
    n_conv = conv_pw1_w.shape[0]
    bf = lambda w: w.astype(BF16)
    layer_w = lambda w, l: bf(w[l:l + 1])
    mem_wk_b, mem_wv_b = bf(mem_wk), bf(mem_wv)
    pw1_b, pw2_b = bf(conv_pw1_w), bf(conv_pw2_w)
    fox_wqt_b, fox_wo_b = bf(fox_wq).transpose(0, 2, 1), bf(fox_wo)
    kvf_b = bf(kvf_w)
    wk_b = kvf_b[:, :D]
    dh = D // FOX_HEADS
    wvt_b = jnp.pad(kvf_b[:, D:2 * D].T.reshape(FOX_HEADS, dh, D),
                    ((0, 0), (0, VT_ROWS - dh), (0, 0))).reshape(FOX_HEADS * VT_ROWS, D)
    vt_ones = jnp.tile((jnp.arange(VT_ROWS) >= dh).astype(F32)[:, None], (FOX_HEADS, LANES))
    wf_b = jnp.pad(kvf_b[:, 2 * D:], ((0, 0), (0, LANES - FOX_HEADS)))
    fb = jnp.pad(fgate_b, (0, LANES - FOX_HEADS)).reshape(1, LANES)

    assert depth == 2 and n_conv == 1 and fox_wq.shape[0] == 1

    mk, mv = _mem_kv(mem, norm_memsrc_g, mem_wk_b, mem_wv_b)
    def tail_args(l):
        return dict(layer=l, gmem=norm_mem_g, wq=layer_w(mem_wq, l), k=mk, v=mv, wo=layer_w(mem_wo, l),
                    gff=norm_ff_g, w1=layer_w(ff_w1, l), w2=layer_w(ff_w2, l))

    conv_params = (norm_mix_g[0], pw1_b[0], conv_pw1_b[0], conv_dw_w[0], conv_dw_b[0],
                   conv_ln_g[0], conv_ln_b[0], pw2_b[0], conv_pw2_b[0])
    x = _mix_tail(x, conv_params, **tail_args(0))
    k, kc, vt, qt, crow = _kvfq(x, kv_norm_g, norm_mix_g[1], wk_b, wvt_b, vt_ones, wf_b, fb, fox_wqt_b, 0)
    o = _fox_attn(qt, k, kc, vt, crow)
    return _tail(x, attn=(o, fox_wo_b), gfin=final_norm_g, **tail_args(1))
```

```python
import functools
import math

import jax
import jax.numpy as jnp
import numpy as np
from jax import lax
from jax.experimental import pallas as pl
from jax.experimental.pallas import tpu as pltpu

F32 = jnp.float32
BF16 = jnp.bfloat16

CONV_WIDTH = 31
FOX_HEADS = 16
MEM_HEADS = 4
RMS_EPS = 1e-6
LN_EPS = 1e-5
LOG2E = 1.4426950408889634

LANES = 128
SUBLANES = 8
HALO = 32
TM = 512
TQ = 512
TK = 256
VMEM_LIMIT = 56 * 1024 * 1024


def _rms(x, g):
    return x * lax.rsqrt(jnp.mean(x * x, axis=-1, keepdims=True) + RMS_EPS) * g


def _mm(a, w):
    return jnp.dot(a.astype(BF16), w, preferred_element_type=F32)


def _mm_nt(a, b):
    return lax.dot_general(a, b, (((1,), (1,)), ((), ())), preferred_element_type=F32)


def _resident(shape, index_map):
    return pl.BlockSpec(shape, index_map, pipeline_mode=pl.Buffered(1))


def _mem_kv_kernel(mem_ref, g_ref, wk_ref, wv_ref, k_ref, v_ref):
    mn = _rms(mem_ref[0], g_ref[0]).astype(BF16)
    k_ref[0, 0] = jnp.dot(mn, wk_ref[0], preferred_element_type=F32).astype(BF16)
    v_ref[0, 0] = jnp.dot(mn, wv_ref[0], preferred_element_type=F32).astype(BF16)


def _mem_kv(mem, g, wk, wv):
    B, M, D = mem.shape
    L = wk.shape[0]
    out = jax.ShapeDtypeStruct((L, B, M, D), BF16)
    return pl.pallas_call(
        _mem_kv_kernel,
        grid=(L, B),
        in_specs=[
            pl.BlockSpec((1, M, D), lambda l, b: (b, 0, 0)),
            pl.BlockSpec((1, 1, D), lambda l, b: (l, 0, 0)),
            pl.BlockSpec((1, D, D), lambda l, b: (l, 0, 0)),
            pl.BlockSpec((1, D, D), lambda l, b: (l, 0, 0)),
        ],
        out_specs=[
            pl.BlockSpec((1, 1, M, D), lambda l, b: (l, b, 0, 0)),
            pl.BlockSpec((1, 1, M, D), lambda l, b: (l, b, 0, 0)),
        ],
        out_shape=[out, out],
        compiler_params=pltpu.CompilerParams(
            dimension_semantics=("arbitrary", "arbitrary"), vmem_limit_bytes=VMEM_LIMIT),
        name="mem_kv",
    )(mem, g.reshape(L, 1, D), wk, wv)


CONV_ROWS = 128
CONV_COLS = LANES


def _conv_mixer_stages(x, g_ref, w1_ref, b1_ref, wdw_ref, bdw_ref, lng_ref, lnb_ref, w2_ref, b2_ref, ubuf, cbuf):
    tm, d = x.shape
    u = _mm(_rms(x, g_ref[...]), w1_ref[...]) + b1_ref[...]
    ubuf[HALO:HALO + tm, :] = u[:, :d] * jax.nn.sigmoid(u[:, d:])
    yield None

    base = HALO - (CONV_WIDTH - 1)
    for r0 in range(0, tm, CONV_ROWS):
        for c0 in range(0, d, CONV_COLS):
            cols = slice(c0, c0 + CONV_COLS)
            win = ubuf[r0:r0 + CONV_ROWS + HALO, cols]
            acc = jnp.broadcast_to(bdw_ref[:, cols], (CONV_ROWS, CONV_COLS))
            for r in range(SUBLANES):
                taps = [o for o in range(base, base + CONV_WIDTH) if o % SUBLANES == r]
                sh = win if r == 0 else pltpu.roll(win, CONV_ROWS + HALO - r, axis=0)
                for o in taps:
                    acc = acc + wdw_ref[o - base:o - base + 1, cols] * sh[o - r:o - r + CONV_ROWS]
            cbuf[r0:r0 + CONV_ROWS, cols] = acc
            yield None
    ubuf[0:HALO, :] = ubuf[tm:tm + HALO, :]

    c = cbuf[...]
    mu = jnp.mean(c, axis=-1, keepdims=True)
    cc = c - mu
    var = jnp.mean(cc * cc, axis=-1, keepdims=True)
    y = cc * lax.rsqrt(var + LN_EPS) * lng_ref[...] + lnb_ref[...]
    y = y * jax.nn.sigmoid(y)
    yield x + _mm(y, w2_ref[...]) + b2_ref[...]


FF_CHUNK = 1024


def _mem_attn_mlp_stages(x, gmem_ref, wq_ref, k_ref, v_ref, wo_ref, gff_ref, w1_ref, w2_ref):
    dh = x.shape[1] // MEM_HEADS
    q = (_mm(_rms(x, gmem_ref[0]), wq_ref[0]) * (1.0 / math.sqrt(dh))).astype(BF16)
    yield None
    heads = []
    for hd in range(MEM_HEADS):
        sl = slice(hd * dh, (hd + 1) * dh)
        s = _mm_nt(q[:, sl], k_ref[0, 0, :, sl])
        p = jnp.exp(s - jnp.max(s, axis=-1, keepdims=True))
        l = jnp.sum(p, axis=-1, keepdims=True)
        heads.append(_mm(p, v_ref[0, 0, :, sl]) / l)
    yield None
    x = x + _mm(jnp.concatenate(heads, axis=-1), wo_ref[0])
    yield None

    hb = _rms(x, gff_ref[0]).astype(BF16)
    acc = jnp.zeros_like(x)
    for c in range(w1_ref.shape[2] // FF_CHUNK):
        sl = slice(c * FF_CHUNK, (c + 1) * FF_CHUNK)
        a = jnp.maximum(jnp.dot(hb, w1_ref[0, :, sl], preferred_element_type=F32), 0.0)
        yield None
        acc = acc + _mm(a * a, w2_ref[0, sl, :])
        yield None
    yield x + acc


def _run(stages):
    for result in stages:
        pass
    return result


def _interleave(a_stages, b_stages, b_per_a):
    ra = rb = None
    a_live = b_live = True
    while a_live or b_live:
        if a_live:
            try:
                ra = next(a_stages)
            except StopIteration:
                a_live = False
        n_b = 0
        while b_live and (n_b < b_per_a or not a_live):
            try:
                rb = next(b_stages)
                n_b += 1
            except StopIteration:
                b_live = False
    return ra, rb


def _tail_kernel(*refs, has_attn, final):
    it = iter(refs)
    x_ref = next(it)
    if has_attn:
        oin_ref, fwo_ref = next(it), next(it)
    tail_refs = [next(it) for _ in range(8)]
    if final:
        gfin_ref = next(it)
    out_ref = next(it)

    x = x_ref[0]
    if has_attn:
        x = x + jnp.dot(oin_ref[0], fwo_ref[0], preferred_element_type=F32)
    x = _run(_mem_attn_mlp_stages(x, *tail_refs))
    if final:
        x = _rms(x, gfin_ref[...])
    out_ref[0] = x


MIXER_STAGES_PER_TAIL_STAGE = 3


def _mix_tail_kernel(x_ref, *refs, tiles_per_seq):
    conv_refs, tail_refs = refs[:9], refs[9:17]
    out_ref, ubuf, cbuf, x1buf = refs[17:]
    t = pl.program_id(0)

    @pl.when(t == 0)
    def _():
        x1buf[...] = jnp.zeros_like(x1buf)

    @pl.when(t % tiles_per_seq == 0)
    def _():
        ubuf[0:HALO, :] = jnp.zeros((HALO, ubuf.shape[1]), F32)

    out, x1_next = _interleave(_mem_attn_mlp_stages(x1buf[...], *tail_refs),
                               _conv_mixer_stages(x_ref[0], *conv_refs, ubuf, cbuf), MIXER_STAGES_PER_TAIL_STAGE)
    out_ref[0] = out
    x1buf[...] = x1_next


def _mix_tail(x, conv_params, layer, gmem, wq, k, v, wo, gff, w1, w2):
    g, cw1, cb1, wdw, bdw, lng, lnb, cw2, cb2 = conv_params
    B, S, D = x.shape
    L = gmem.shape[0]
    M = k.shape[2]
    F = w1.shape[2]
    n_seq = S // TM
    n_tiles = B * n_seq
    row = lambda v: v.reshape(1, -1)
    const = lambda t: (0, 0)
    lay = lambda t: (layer, 0, 0)
    first = lambda t: (0, 0, 0)

    def mixer_tile(t):
        t = jnp.minimum(t, n_tiles - 1)
        return t // n_seq, t % n_seq, 0

    def tail_tile(t):
        t = jnp.maximum(t - 1, 0)
        return t // n_seq, t % n_seq, 0

    mem_block = pl.BlockSpec((1, 1, M, D), lambda t: (layer, jnp.maximum(t - 1, 0) // n_seq, 0, 0))
    args = (x, row(g), cw1, row(cb1), wdw, row(bdw), row(lng), row(lnb), cw2, row(cb2),
            gmem.reshape(L, 1, D), wq, k, v, wo, gff.reshape(L, 1, D), w1, w2)
    weights = (cw1, cw2, wq, wo, w1, w2)
    return pl.pallas_call(
        functools.partial(_mix_tail_kernel, tiles_per_seq=n_seq),
        grid=(n_tiles + 1,),
        in_specs=[
            pl.BlockSpec((1, TM, D), mixer_tile),
            _resident((1, D), const),
            _resident((D, 2 * D), const),
            _resident((1, 2 * D), const),
            _resident((CONV_WIDTH, D), const),
            _resident((1, D), const),
            _resident((1, D), const),
            _resident((1, D), const),
            _resident((D, D), const),
            _resident((1, D), const),
            _resident((1, 1, D), lay),
            _resident((1, D, D), first),
            mem_block,
            mem_block,
            _resident((1, D, D), first),
            _resident((1, 1, D), lay),
            _resident((1, D, F), first),
            _resident((1, F, D), first),
        ],
        out_specs=pl.BlockSpec((1, TM, D), tail_tile),
        out_shape=jax.ShapeDtypeStruct((B, S, D), F32),
        scratch_shapes=[pltpu.VMEM((TM + HALO, D), F32), pltpu.VMEM((TM, D), F32), pltpu.VMEM((TM, D), F32)],
        compiler_params=pltpu.CompilerParams(
            dimension_semantics=("arbitrary",), vmem_limit_bytes=VMEM_LIMIT,
            allow_input_fusion=[any(a is w for w in weights) for a in args]),
        name="mix_tail",
    )(*args)


def _tail(x, layer, gmem, wq, k, v, wo, gff, w1, w2, attn=None, gfin=None):
    B, S, D = x.shape
    L = gmem.shape[0]
    M = k.shape[2]
    F = w1.shape[2]
    lay = lambda b, s: (layer, 0, 0)
    first = lambda b, s: (0, 0, 0)
    tile = pl.BlockSpec((1, TM, D), lambda b, s: (b, s, 0))
    args, specs = [x], [tile]
    if attn is not None:
        o_in, fwo = attn
        args += [o_in, fwo]
        specs += [pl.BlockSpec((1, TM, D), lambda b, s: (b, s, 0)), _resident((1, D, D), first)]
    args += [gmem.reshape(L, 1, D), wq, k, v, wo, gff.reshape(L, 1, D), w1, w2]
    specs += [
        _resident((1, 1, D), lay),
        _resident((1, D, D), first),
        pl.BlockSpec((1, 1, M, D), lambda b, s: (layer, b, 0, 0)),
        pl.BlockSpec((1, 1, M, D), lambda b, s: (layer, b, 0, 0)),
        _resident((1, D, D), first),
        _resident((1, 1, D), lay),
        _resident((1, D, F), first),
        _resident((1, F, D), first),
    ]
    if gfin is not None:
        args.append(gfin.reshape(1, D))
        specs.append(_resident((1, D), lambda b, s: (0, 0)))
    weights = (wq, wo, w1, w2) + (() if attn is None else (attn[1],))
    return pl.pallas_call(
        functools.partial(_tail_kernel, has_attn=attn is not None, final=gfin is not None),
        grid=(B, S // TM),
        in_specs=specs,
        out_specs=tile,
        out_shape=jax.ShapeDtypeStruct((B, S, D), F32),
        compiler_params=pltpu.CompilerParams(
            dimension_semantics=("arbitrary", "arbitrary"), vmem_limit_bytes=VMEM_LIMIT,
            allow_input_fusion=[any(a is w for w in weights) for a in args]),
        name=f"tail{layer}",
    )(*args)


VT_ROWS = 80
C_PIECES = 3


def _bf16_pieces(x):
    pieces, rest = [], x
    for _ in range(C_PIECES):
        piece = rest.astype(BF16)
        pieces.append(piece)
        rest = rest - piece.astype(F32)
    return jnp.concatenate(pieces, axis=1)


def _kvfq_kernel(x_ref, gkv_ref, gq_ref, wk_ref, wvt_ref, ones_ref, wf_ref, fb_ref, place_ref, wq_ref,
                 k_ref, kc_ref, vt_ref, q_ref, crow_ref, carry):
    tm = x_ref.shape[1]
    dh = x_ref.shape[2] // FOX_HEADS

    @pl.when(pl.program_id(1) == 0)
    def _():
        carry[...] = jnp.zeros_like(carry)

    x = x_ref[0]
    hk = _rms(x, gkv_ref[...]).astype(BF16)
    k_ref[0] = jnp.dot(hk, wk_ref[...], preferred_element_type=F32).astype(BF16)
    vt_ref[0] = (_mm_nt(wvt_ref[...], hk) + jnp.tile(ones_ref[...], (1, tm // LANES))).astype(BF16)

    f = jnp.dot(hk, wf_ref[...], preferred_element_type=F32) + fb_ref[...]
    log_f = jnp.minimum(f, 0.0) - jnp.log(1.0 + jnp.exp(-jnp.abs(f)))
    tri = (lax.broadcasted_iota(jnp.int32, (tm, tm), 1)
           <= lax.broadcasted_iota(jnp.int32, (tm, tm), 0)).astype(BF16)
    parts = jnp.dot(tri, _bf16_pieces(log_f), preferred_element_type=F32)
    c = sum(parts[:, i * LANES:(i + 1) * LANES] for i in range(C_PIECES)) + carry[0:1, :]
    carry[0:1, :] = c[tm - 1:tm, :]
    c2 = c * LOG2E
    crow_ref[0] = c2.T[0:FOX_HEADS, :]

    kc_ref[0] = jnp.dot(_bf16_pieces(c2), place_ref[...], preferred_element_type=F32).astype(BF16)

    hq = _rms(x, gq_ref[...]).astype(BF16)
    q_ref[0] = (_mm_nt(wq_ref[0], hq) * (LOG2E / math.sqrt(dh))).astype(BF16)


def _gate_placement(n_heads):
    place = np.zeros((C_PIECES * LANES, n_heads // 2 * LANES), np.float32)
    for h in range(n_heads):
        for i in range(C_PIECES):
            place[i * LANES + h, (h // 2) * LANES + (h % 2) * C_PIECES + i] = 1.0
    return jnp.asarray(place, BF16)


def _kvfq(x, gkv, gq, wk, wvt, vt_ones, wf, fb, wq, layer):
    place = _gate_placement(FOX_HEADS)
    B, S, D = x.shape
    R = wvt.shape[0]
    const = lambda b, s: (0, 0)
    tile = pl.BlockSpec((1, TM, D), lambda b, s: (b, s, 0))
    return pl.pallas_call(
        _kvfq_kernel,
        grid=(B, S // TM),
        in_specs=[
            tile,
            _resident((1, D), const),
            _resident((1, D), const),
            _resident((D, D), const),
            _resident((R, D), const),
            _resident((R, LANES), const),
            _resident((D, LANES), const),
            _resident((1, LANES), const),
            _resident(place.shape, const),
            _resident((1, D, D), lambda b, s: (layer, 0, 0)),
        ],
        out_specs=[
            tile,
            tile,
            pl.BlockSpec((1, R, TM), lambda b, s: (b, 0, s)),
            pl.BlockSpec((1, D, TM), lambda b, s: (b, 0, s)),
            pl.BlockSpec((1, FOX_HEADS, TM), lambda b, s: (b, 0, s)),
        ],
        out_shape=[
            jax.ShapeDtypeStruct((B, S, D), BF16),
            jax.ShapeDtypeStruct((B, S, D), BF16),
            jax.ShapeDtypeStruct((B, R, S), BF16),
            jax.ShapeDtypeStruct((B, D, S), BF16),
            jax.ShapeDtypeStruct((B, FOX_HEADS, S), F32),
        ],
        scratch_shapes=[pltpu.VMEM((8, LANES), F32)],
        compiler_params=pltpu.CompilerParams(
            dimension_semantics=("arbitrary", "arbitrary"), vmem_limit_bytes=VMEM_LIMIT),
        name="kvfq",
    )(x, gkv.reshape(1, D), gq.reshape(1, D), wk, wvt, vt_ones, wf, fb, place, wq)


FOX_GROUP = 8
FOX_LAG = 2


def _fox_kernel(q_ref, k_ref, kc_ref, vt_ref, crow_ref, o_ref, tbuf, acc):
    tq = q_ref.shape[2]
    dh = LANES // 2
    grp = pl.program_id(1)
    qi = pl.program_id(2)
    G = FOX_GROUP

    tk = tbuf.shape[1]
    n_sub = tq // tk
    row = lax.broadcasted_iota(jnp.int32, (LANES, tq), 0)
    key_minus_query = (lax.broadcasted_iota(jnp.int32, (tk, tq), 0) - lax.broadcasted_iota(jnp.int32, (tk, tq), 1))
    qx, cqs = [], []
    for h in range(G):
        pr, hh = divmod(h, 2)
        qp = q_ref[0, pr * LANES:(pr + 1) * LANES, :]
        gate = (row >= hh * C_PIECES) & (row < (hh + 1) * C_PIECES)
        qx.append(jnp.concatenate([jnp.where(row // dh == hh, qp, jnp.zeros_like(qp)),
                                   jnp.where(gate, -1.0, 0.0).astype(BF16)], axis=0))
        cqs.append(crow_ref[0, pl.ds(G * grp + h, 1), :])
        acc[h] = jnp.zeros(acc.shape[1:], F32)

    def scores(kb, h, diag_offset):
        pr = h // 2
        col0 = diag_offset or 0
        k0 = pl.multiple_of(kb * tk, tk)
        kx = jnp.concatenate([k_ref[0, pl.ds(k0, tk), pr * LANES:(pr + 1) * LANES],
                              kc_ref[0, pl.ds(k0, tk), pr * LANES:(pr + 1) * LANES]], axis=1)
        t = jnp.dot(kx, qx[h][:, col0:], preferred_element_type=F32)
        if diag_offset is not None:
            t = jnp.where(key_minus_query[:, col0:] <= -diag_offset, t, -jnp.inf)
        tbuf[h, :, col0:] = t
        return jnp.max(t, axis=0, keepdims=True)

    def consume(kb, h, mb, m, col0):
        k0 = pl.multiple_of(kb * tk, tk)
        cq, m_old = cqs[h][:, col0:], m[:, col0:]
        m_new = jnp.maximum(m_old, mb + cq)
        p = jnp.exp2(tbuf[h, :, col0:] + (cq - m_new)).astype(BF16)
        v = vt_ref[0, h * VT_ROWS:(h + 1) * VT_ROWS, pl.ds(k0, tk)]
        acc[h, :, col0:] = jnp.exp2(m_old - m_new) * acc[h, :, col0:] + jnp.dot(v, p, preferred_element_type=F32)
        return m_new if col0 == 0 else jnp.concatenate([m[:, :col0], m_new], axis=1)

    def sweep(kb, state, diag_offset=None, pending_col0=0):
        kb_prev, pending, ms = state
        ms, mbs = list(ms), []
        for h in range(G):
            mbs.append(scores(kb, h, diag_offset))
            if h >= FOX_LAG:
                ms[h - FOX_LAG] = consume(kb, h - FOX_LAG, mbs[h - FOX_LAG], ms[h - FOX_LAG], diag_offset or 0)
            elif pending is not None:
                hp = h + G - FOX_LAG
                ms[hp] = consume(kb_prev, hp, pending[h], ms[hp], pending_col0)
        return kb, tuple(mbs[G - FOX_LAG:]), tuple(ms)

    def step(j, state):
        for i in range(n_sub):
            state = sweep(n_sub * j + i, state)
        return state

    state = (None, None, (jnp.full((1, tq), -jnp.inf, F32),) * G)
    pending_col0 = 0
    for i in reversed(range(n_sub)):
        state = sweep(n_sub * qi + i, state, diag_offset=i * tk, pending_col0=pending_col0)
        pending_col0 = i * tk
    kb_prev, pending, ms = lax.fori_loop(0, qi, step, state)
    for h in range(FOX_LAG):
        hp = h + G - FOX_LAG
        consume(kb_prev, hp, pending[h], ms[hp], 0)
    o_ref[0] = jnp.concatenate([acc[h, 0:dh] / acc[h, dh:dh + 1] for h in range(G)], axis=0).T.astype(BF16)


def _fox_attn(qt, k, kc, vt, crow):
    B, S, D = k.shape
    G = FOX_GROUP
    W = G * (LANES // 2)
    return pl.pallas_call(
        _fox_kernel,
        grid=(B, D // W, S // TQ),
        in_specs=[
            pl.BlockSpec((1, W, TQ), lambda b, g, i: (b, g, i)),
            pl.BlockSpec((1, S, W), lambda b, g, i: (b, 0, g)),
            pl.BlockSpec((1, S, W), lambda b, g, i: (b, 0, g)),
            pl.BlockSpec((1, G * VT_ROWS, S), lambda b, g, i: (b, g, 0)),
            pl.BlockSpec((1, FOX_HEADS, TQ), lambda b, g, i: (b, 0, i)),
        ],
        out_specs=pl.BlockSpec((1, TQ, W), lambda b, g, i: (b, i, g)),
        out_shape=jax.ShapeDtypeStruct((B, S, D), BF16),
        scratch_shapes=[pltpu.VMEM((G, TK, TQ), F32), pltpu.VMEM((G, VT_ROWS, TQ), F32)],
        compiler_params=pltpu.CompilerParams(
            dimension_semantics=("arbitrary", "arbitrary", "arbitrary"), vmem_limit_bytes=VMEM_LIMIT),
        name="fox_attn",
    )(qt, k, kc, vt, crow)


def kernel(x, mem, norm_mix_g, norm_mem_g, norm_memsrc_g, norm_ff_g, mem_wq, mem_wk, mem_wv, mem_wo, ff_w1, ff_w2,
           conv_pw1_w, conv_pw1_b, conv_dw_w, conv_dw_b, conv_ln_g, conv_ln_b, conv_pw2_w, conv_pw2_b,
           kv_norm_g, kvf_w, fgate_b, fox_wq, fox_wo, final_norm_g):
    D = x.shape[2]
    depth = norm_mix_g.shape[0]
    n_conv = conv_pw1_w.shape[0]
    bf = lambda w: w.astype(BF16)
    layer_w = lambda w, l: bf(w[l:l + 1])
    mem_wk_b, mem_wv_b = bf(mem_wk), bf(mem_wv)
    pw1_b, pw2_b = bf(conv_pw1_w), bf(conv_pw2_w)
    fox_wqt_b, fox_wo_b = bf(fox_wq).transpose(0, 2, 1), bf(fox_wo)
    kvf_b = bf(kvf_w)
    wk_b = kvf_b[:, :D]
    dh = D // FOX_HEADS
    wvt_b = jnp.pad(kvf_b[:, D:2 * D].T.reshape(FOX_HEADS, dh, D),
                    ((0, 0), (0, VT_ROWS - dh), (0, 0))).reshape(FOX_HEADS * VT_ROWS, D)
    vt_ones = jnp.tile((jnp.arange(VT_ROWS) >= dh).astype(F32)[:, None], (FOX_HEADS, LANES))
    wf_b = jnp.pad(kvf_b[:, 2 * D:], ((0, 0), (0, LANES - FOX_HEADS)))
    fb = jnp.pad(fgate_b, (0, LANES - FOX_HEADS)).reshape(1, LANES)

    assert depth == 2 and n_conv == 1 and fox_wq.shape[0] == 1

    mk, mv = _mem_kv(mem, norm_memsrc_g, mem_wk_b, mem_wv_b)
    def tail_args(l):
        return dict(layer=l, gmem=norm_mem_g, wq=layer_w(mem_wq, l), k=mk, v=mv, wo=layer_w(mem_wo, l),
                    gff=norm_ff_g, w1=layer_w(ff_w1, l), w2=layer_w(ff_w2, l))

    conv_params = (norm_mix_g[0], pw1_b[0], conv_pw1_b[0], conv_dw_w[0], conv_dw_b[0],
                   conv_ln_g[0], conv_ln_b[0], pw2_b[0], conv_pw2_b[0])
    x = _mix_tail(x, conv_params, **tail_args(0))
    k, kc, vt, qt, crow = _kvfq(x, kv_norm_g, norm_mix_g[1], wk_b, wvt_b, vt_ones, wf_b, fb, fox_wqt_b, 0)
    o = _fox_attn(qt, k, kc, vt, crow)
    return _tail(x, attn=(o, fox_wo_b), gfin=final_norm_g, **tail_args(1))
```

```python
import functools
import math

import jax
import jax.numpy as jnp
import numpy as np
from jax import lax
from jax.experimental import pallas as pl
from jax.experimental.pallas import tpu as pltpu

F32 = jnp.float32
BF16 = jnp.bfloat16

CONV_WIDTH = 31
FOX_HEADS = 16
MEM_HEADS = 4
RMS_EPS = 1e-6
LN_EPS = 1e-5
LOG2E = 1.4426950408889634

LANES = 128
SUBLANES = 8
HALO = 32
TM = 512
TQ = 512
TK = 256
VMEM_LIMIT = 56 * 1024 * 1024


def _rms(x, g):
    return x * lax.rsqrt(jnp.mean(x * x, axis=-1, keepdims=True) + RMS_EPS) * g


def _mm(a, w):
    return jnp.dot(a.astype(BF16), w, preferred_element_type=F32)


def _mm_nt(a, b):
    return lax.dot_general(a, b, (((1,), (1,)), ((), ())), preferred_element_type=F32)


def _resident(shape, index_map):
    return pl.BlockSpec(shape, index_map, pipeline_mode=pl.Buffered(1))


def _mem_kv_kernel(mem_ref, g_ref, wk_ref, wv_ref, k_ref, v_ref):
    mn = _rms(mem_ref[0], g_ref[0]).astype(BF16)
    k_ref[0, 0] = jnp.dot(mn, wk_ref[0], preferred_element_type=F32).astype(BF16)
    v_ref[0, 0] = jnp.dot(mn, wv_ref[0], preferred_element_type=F32).astype(BF16)


def _mem_kv(mem, g, wk, wv):
    B, M, D = mem.shape
    L = wk.shape[0]
    out = jax.ShapeDtypeStruct((L, B, M, D), BF16)
    return pl.pallas_call(
        _mem_kv_kernel,
        grid=(L, B),
        in_specs=[
            pl.BlockSpec((1, M, D), lambda l, b: (b, 0, 0)),
            pl.BlockSpec((1, 1, D), lambda l, b: (l, 0, 0)),
            pl.BlockSpec((1, D, D), lambda l, b: (l, 0, 0)),
            pl.BlockSpec((1, D, D), lambda l, b: (l, 0, 0)),
        ],
        out_specs=[
            pl.BlockSpec((1, 1, M, D), lambda l, b: (l, b, 0, 0)),
            pl.BlockSpec((1, 1, M, D), lambda l, b: (l, b, 0, 0)),
        ],
        out_shape=[out, out],
        compiler_params=pltpu.CompilerParams(
            dimension_semantics=("arbitrary", "arbitrary"), vmem_limit_bytes=VMEM_LIMIT),
        name="mem_kv",
    )(mem, g.reshape(L, 1, D), wk, wv)


CONV_ROWS = 128
CONV_COLS = LANES


def _conv_mixer_stages(x, g_ref, w1_ref, b1_ref, wdw_ref, bdw_ref, lng_ref, lnb_ref, w2_ref, b2_ref, ubuf, cbuf):
    tm, d = x.shape
    u = _mm(_rms(x, g_ref[...]), w1_ref[...]) + b1_ref[...]
    ubuf[HALO:HALO + tm, :] = u[:, :d] * jax.nn.sigmoid(u[:, d:])
    yield None

    base = HALO - (CONV_WIDTH - 1)
    for r0 in range(0, tm, CONV_ROWS):
        for c0 in range(0, d, CONV_COLS):
            cols = slice(c0, c0 + CONV_COLS)
            win = ubuf[r0:r0 + CONV_ROWS + HALO, cols]
            acc = jnp.broadcast_to(bdw_ref[:, cols], (CONV_ROWS, CONV_COLS))
            for r in range(SUBLANES):
                taps = [o for o in range(base, base + CONV_WIDTH) if o % SUBLANES == r]
                sh = win if r == 0 else pltpu.roll(win, CONV_ROWS + HALO - r, axis=0)
                for o in taps:
                    acc = acc + wdw_ref[o - base:o - base + 1, cols] * sh[o - r:o - r + CONV_ROWS]
            cbuf[r0:r0 + CONV_ROWS, cols] = acc
            yield None
    ubuf[0:HALO, :] = ubuf[tm:tm + HALO, :]

    c = cbuf[...]
    mu = jnp.mean(c, axis=-1, keepdims=True)
    cc = c - mu
    var = jnp.mean(cc * cc, axis=-1, keepdims=True)
    y = cc * lax.rsqrt(var + LN_EPS) * lng_ref[...] + lnb_ref[...]
    y = y * jax.nn.sigmoid(y)
    yield x + _mm(y, w2_ref[...]) + b2_ref[...]


FF_CHUNK = 1024


def _mem_attn_mlp_stages(x, gmem_ref, wq_ref, k_ref, v_ref, wo_ref, gff_ref, w1_ref, w2_ref):
    dh = x.shape[1] // MEM_HEADS
    q = (_mm(_rms(x, gmem_ref[0]), wq_ref[0]) * (1.0 / math.sqrt(dh))).astype(BF16)
    yield None
    heads = []
    for hd in range(MEM_HEADS):
        sl = slice(hd * dh, (hd + 1) * dh)
        s = _mm_nt(q[:, sl], k_ref[0, 0, :, sl])
        p = jnp.exp(s - jnp.max(s, axis=-1, keepdims=True))
        l = jnp.sum(p, axis=-1, keepdims=True)
        heads.append(_mm(p, v_ref[0, 0, :, sl]) / l)
    yield None
    x = x + _mm(jnp.concatenate(heads, axis=-1), wo_ref[0])
    yield None

    hb = _rms(x, gff_ref[0]).astype(BF16)
    acc = jnp.zeros_like(x)
    for c in range(w1_ref.shape[2] // FF_CHUNK):
        sl = slice(c * FF_CHUNK, (c + 1) * FF_CHUNK)
        a = jnp.maximum(jnp.dot(hb, w1_ref[0, :, sl], preferred_element_type=F32), 0.0)
        yield None
        acc = acc + _mm(a * a, w2_ref[0, sl, :])
        yield None
    yield x + acc


def _run(stages):
    for result in stages:
        pass
    return result


def _interleave(a_stages, b_stages, b_per_a):
    ra = rb = None
    a_live = b_live = True
    while a_live or b_live:
        if a_live:
            try:
                ra = next(a_stages)
            except StopIteration:
                a_live = False
        n_b = 0
        while b_live and (n_b < b_per_a or not a_live):
            try:
                rb = next(b_stages)
                n_b += 1
            except StopIteration:
                b_live = False
    return ra, rb


def _tail_kernel(*refs, has_attn, final):
    it = iter(refs)
    x_ref = next(it)
    if has_attn:
        oin_ref, fwo_ref = next(it), next(it)
    tail_refs = [next(it) for _ in range(8)]
    if final:
        gfin_ref = next(it)
    out_ref = next(it)

    x = x_ref[0]
    if has_attn:
        x = x + jnp.dot(oin_ref[0], fwo_ref[0], preferred_element_type=F32)
    x = _run(_mem_attn_mlp_stages(x, *tail_refs))
    if final:
        x = _rms(x, gfin_ref[...])
    out_ref[0] = x


MIXER_STAGES_PER_TAIL_STAGE = 3


def _mix_tail_kernel(x_ref, *refs, tiles_per_seq):
    conv_refs, tail_refs = refs[:9], refs[9:17]
    out_ref, ubuf, cbuf, x1buf = refs[17:]
    t = pl.program_id(0)

    @pl.when(t == 0)
    def _():
        x1buf[...] = jnp.zeros_like(x1buf)

    @pl.when(t % tiles_per_seq == 0)
    def _():
        ubuf[0:HALO, :] = jnp.zeros((HALO, ubuf.shape[1]), F32)

    out, x1_next = _interleave(_mem_attn_mlp_stages(x1buf[...], *tail_refs),
                               _conv_mixer_stages(x_ref[0], *conv_refs, ubuf, cbuf), MIXER_STAGES_PER_TAIL_STAGE)
    out_ref[0] = out
    x1buf[...] = x1_next


def _mix_tail(x, conv_params, layer, gmem, wq, k, v, wo, gff, w1, w2):
    g, cw1, cb1, wdw, bdw, lng, lnb, cw2, cb2 = conv_params
    B, S, D = x.shape
    L = gmem.shape[0]
    M = k.shape[2]
    F = w1.shape[2]
    n_seq = S // TM
    n_tiles = B * n_seq
    row = lambda v: v.reshape(1, -1)
    const = lambda t: (0, 0)
    lay = lambda t: (layer, 0, 0)
    first = lambda t: (0, 0, 0)

    def mixer_tile(t):
        t = jnp.minimum(t, n_tiles - 1)
        return t // n_seq, t % n_seq, 0

    def tail_tile(t):
        t = jnp.maximum(t - 1, 0)
        return t // n_seq, t % n_seq, 0

    mem_block = pl.BlockSpec((1, 1, M, D), lambda t: (layer, jnp.maximum(t - 1, 0) // n_seq, 0, 0))
    args = (x, row(g), cw1, row(cb1), wdw, row(bdw), row(lng), row(lnb), cw2, row(cb2),
            gmem.reshape(L, 1, D), wq, k, v, wo, gff.reshape(L, 1, D), w1, w2)
    weights = (cw1, cw2, wq, wo, w1, w2)
    return pl.pallas_call(
        functools.partial(_mix_tail_kernel, tiles_per_seq=n_seq),
        grid=(n_tiles + 1,),
        in_specs=[
            pl.BlockSpec((1, TM, D), mixer_tile),
            _resident((1, D), const),
            _resident((D, 2 * D), const),
            _resident((1, 2 * D), const),
            _resident((CONV_WIDTH, D), const),
            _resident((1, D), const),
            _resident((1, D), const),
            _resident((1, D), const),
            _resident((D, D), const),
            _resident((1, D), const),
            _resident((1, 1, D), lay),
            _resident((1, D, D), first),
            mem_block,
            mem_block,
            _resident((1, D, D), first),
            _resident((1, 1, D), lay),
            _resident((1, D, F), first),
            _resident((1, F, D), first),
        ],
        out_specs=pl.BlockSpec((1, TM, D), tail_tile),
        out_shape=jax.ShapeDtypeStruct((B, S, D), F32),
        scratch_shapes=[pltpu.VMEM((TM + HALO, D), F32), pltpu.VMEM((TM, D), F32), pltpu.VMEM((TM, D), F32)],
        compiler_params=pltpu.CompilerParams(
            dimension_semantics=("arbitrary",), vmem_limit_bytes=VMEM_LIMIT,
            allow_input_fusion=[any(a is w for w in weights) for a in args]),
        name="mix_tail",
    )(*args)


def _tail(x, layer, gmem, wq, k, v, wo, gff, w1, w2, attn=None, gfin=None):
    B, S, D = x.shape
    L = gmem.shape[0]
    M = k.shape[2]
    F = w1.shape[2]
    lay = lambda b, s: (layer, 0, 0)
    first = lambda b, s: (0, 0, 0)
    tile = pl.BlockSpec((1, TM, D), lambda b, s: (b, s, 0))
    args, specs = [x], [tile]
    if attn is not None:
        o_in, fwo = attn
        args += [o_in, fwo]
        specs += [pl.BlockSpec((1, TM, D), lambda b, s: (b, s, 0)), _resident((1, D, D), first)]
    args += [gmem.reshape(L, 1, D), wq, k, v, wo, gff.reshape(L, 1, D), w1, w2]
    specs += [
        _resident((1, 1, D), lay),
        _resident((1, D, D), first),
        pl.BlockSpec((1, 1, M, D), lambda b, s: (layer, b, 0, 0)),
        pl.BlockSpec((1, 1, M, D), lambda b, s: (layer, b, 0, 0)),
        _resident((1, D, D), first),
        _resident((1, 1, D), lay),
        _resident((1, D, F), first),
        _resident((1, F, D), first),
    ]
    if gfin is not None:
        args.append(gfin.reshape(1, D))
        specs.append(_resident((1, D), lambda b, s: (0, 0)))
    weights = (wq, wo, w1, w2) + (() if attn is None else (attn[1],))
    return pl.pallas_call(
        functools.partial(_tail_kernel, has_attn=attn is not None, final=gfin is not None),
        grid=(B, S // TM),
        in_specs=specs,
        out_specs=tile,
        out_shape=jax.ShapeDtypeStruct((B, S, D), F32),
        compiler_params=pltpu.CompilerParams(
            dimension_semantics=("arbitrary", "arbitrary"), vmem_limit_bytes=VMEM_LIMIT,
            allow_input_fusion=[any(a is w for w in weights) for a in args]),
        name=f"tail{layer}",
    )(*args)


C_PIECES = 3


def _bf16_pieces(x):
    pieces, rest = [], x
    for _ in range(C_PIECES):
        piece = rest.astype(BF16)
        pieces.append(piece)
        rest = rest - piece.astype(F32)
    return jnp.concatenate(pieces, axis=1)


def _kvfq_kernel(x_ref, gkv_ref, gq_ref, wk_ref, wvt_ref, wf_ref, fb_ref, place_ref, wq_ref,
                 k_ref, kc_ref, vt_ref, q_ref, crow_ref, carry):
    tm = x_ref.shape[1]
    dh = x_ref.shape[2] // FOX_HEADS

    @pl.when(pl.program_id(1) == 0)
    def _():
        carry[...] = jnp.zeros_like(carry)

    x = x_ref[0]
    hk = _rms(x, gkv_ref[...]).astype(BF16)
    k_ref[0] = jnp.dot(hk, wk_ref[...], preferred_element_type=F32).astype(BF16)
    vt_ref[0] = _mm_nt(wvt_ref[...], hk).astype(BF16)

    f = jnp.dot(hk, wf_ref[...], preferred_element_type=F32) + fb_ref[...]
    log_f = jnp.minimum(f, 0.0) - jnp.log(1.0 + jnp.exp(-jnp.abs(f)))
    tri = (lax.broadcasted_iota(jnp.int32, (tm, tm), 1)
           <= lax.broadcasted_iota(jnp.int32, (tm, tm), 0)).astype(BF16)
    parts = jnp.dot(tri, _bf16_pieces(log_f), preferred_element_type=F32)
    c = sum(parts[:, i * LANES:(i + 1) * LANES] for i in range(C_PIECES)) + carry[0:1, :]
    carry[0:1, :] = c[tm - 1:tm, :]
    c2 = c * LOG2E
    crow_ref[0] = c2.T[0:FOX_HEADS, :]

    kc_ref[0] = jnp.dot(_bf16_pieces(c2), place_ref[...], preferred_element_type=F32).astype(BF16)

    hq = _rms(x, gq_ref[...]).astype(BF16)
    q_ref[0] = (_mm_nt(wq_ref[0], hq) * (LOG2E / math.sqrt(dh))).astype(BF16)


def _gate_placement(n_heads):
    place = np.zeros((C_PIECES * LANES, n_heads // 2 * LANES), np.float32)
    for h in range(n_heads):
        for i in range(C_PIECES):
            place[i * LANES + h, (h // 2) * LANES + (h % 2) * C_PIECES + i] = 1.0
    return jnp.asarray(place, BF16)


def _kvfq(x, gkv, gq, wk, wvt, wf, fb, wq, layer):
    place = _gate_placement(FOX_HEADS)
    B, S, D = x.shape
    const = lambda b, s: (0, 0)
    tile = pl.BlockSpec((1, TM, D), lambda b, s: (b, s, 0))
    return pl.pallas_call(
        _kvfq_kernel,
        grid=(B, S // TM),
        in_specs=[
            tile,
            _resident((1, D), const),
            _resident((1, D), const),
            _resident((D, D), const),
            _resident((D, D), const),
            _resident((D, LANES), const),
            _resident((1, LANES), const),
            _resident(place.shape, const),
            _resident((1, D, D), lambda b, s: (layer, 0, 0)),
        ],
        out_specs=[
            tile,
            tile,
            pl.BlockSpec((1, D, TM), lambda b, s: (b, 0, s)),
            pl.BlockSpec((1, D, TM), lambda b, s: (b, 0, s)),
            pl.BlockSpec((1, FOX_HEADS, TM), lambda b, s: (b, 0, s)),
        ],
        out_shape=[
            jax.ShapeDtypeStruct((B, S, D), BF16),
            jax.ShapeDtypeStruct((B, S, D), BF16),
            jax.ShapeDtypeStruct((B, D, S), BF16),
            jax.ShapeDtypeStruct((B, D, S), BF16),
            jax.ShapeDtypeStruct((B, FOX_HEADS, S), F32),
        ],
        scratch_shapes=[pltpu.VMEM((8, LANES), F32)],
        compiler_params=pltpu.CompilerParams(
            dimension_semantics=("arbitrary", "arbitrary"), vmem_limit_bytes=VMEM_LIMIT),
        name="kvfq",
    )(x, gkv.reshape(1, D), gq.reshape(1, D), wk, wvt, wf, fb, place, wq)


FOX_GROUP = 8
FOX_LAG = 2


def _fox_kernel(q_ref, k_ref, kc_ref, vt_ref, crow_ref, o_ref, tbuf, acc):
    tq = q_ref.shape[2]
    dh = LANES // 2
    grp = pl.program_id(1)
    qi = pl.program_id(2)
    G = FOX_GROUP

    tk = tbuf.shape[1]
    n_sub = tq // tk
    row = lax.broadcasted_iota(jnp.int32, (LANES, tq), 0)
    key_minus_query = (lax.broadcasted_iota(jnp.int32, (tk, tq), 0) - lax.broadcasted_iota(jnp.int32, (tk, tq), 1))
    qx, cqs = [], []
    for h in range(G):
        pr, hh = divmod(h, 2)
        qp = q_ref[0, pr * LANES:(pr + 1) * LANES, :]
        gate = (row >= hh * C_PIECES) & (row < (hh + 1) * C_PIECES)
        qx.append(jnp.concatenate([jnp.where(row // dh == hh, qp, jnp.zeros_like(qp)),
                                   jnp.where(gate, -1.0, 0.0).astype(BF16)], axis=0))
        cqs.append(crow_ref[0, pl.ds(G * grp + h, 1), :])
        acc[h] = jnp.zeros(acc.shape[1:], F32)

    def scores(kb, h, diag_offset):
        pr = h // 2
        col0 = diag_offset or 0
        k0 = pl.multiple_of(kb * tk, tk)
        kx = jnp.concatenate([k_ref[0, pl.ds(k0, tk), pr * LANES:(pr + 1) * LANES],
                              kc_ref[0, pl.ds(k0, tk), pr * LANES:(pr + 1) * LANES]], axis=1)
        t = jnp.dot(kx, qx[h][:, col0:], preferred_element_type=F32)
        if diag_offset is not None:
            t = jnp.where(key_minus_query[:, col0:] <= -diag_offset, t, -jnp.inf)
        tbuf[h, :, col0:] = t
        return jnp.max(t, axis=0, keepdims=True)

    def consume(kb, h, mb, m, col0):
        k0 = pl.multiple_of(kb * tk, tk)
        cq, m_old = cqs[h][:, col0:], m[:, col0:]
        m_new = jnp.maximum(m_old, mb + cq)
        p = jnp.exp2(tbuf[h, :, col0:] + (cq - m_new))
        v = vt_ref[0, h * dh:(h + 1) * dh, pl.ds(k0, tk)]
        scaled = jnp.exp2(m_old - m_new) * acc[h, :, col0:]
        acc[h, 0:dh, col0:] = scaled[0:dh] + jnp.dot(v, p.astype(BF16), preferred_element_type=F32)
        acc[h, dh:dh + 1, col0:] = scaled[dh:dh + 1] + jnp.sum(p, axis=0, keepdims=True)
        return m_new if col0 == 0 else jnp.concatenate([m[:, :col0], m_new], axis=1)

    def sweep(kb, state, diag_offset=None, pending_col0=0):
        kb_prev, pending, ms = state
        ms, mbs = list(ms), []
        for h in range(G):
            mbs.append(scores(kb, h, diag_offset))
            if h >= FOX_LAG:
                ms[h - FOX_LAG] = consume(kb, h - FOX_LAG, mbs[h - FOX_LAG], ms[h - FOX_LAG], diag_offset or 0)
            elif pending is not None:
                hp = h + G - FOX_LAG
                ms[hp] = consume(kb_prev, hp, pending[h], ms[hp], pending_col0)
        return kb, tuple(mbs[G - FOX_LAG:]), tuple(ms)

    def step(j, state):
        for i in range(n_sub):
            state = sweep(n_sub * j + i, state)
        return state

    state = (None, None, (jnp.full((1, tq), -jnp.inf, F32),) * G)
    pending_col0 = 0
    for i in reversed(range(n_sub)):
        state = sweep(n_sub * qi + i, state, diag_offset=i * tk, pending_col0=pending_col0)
        pending_col0 = i * tk
    kb_prev, pending, ms = lax.fori_loop(0, qi, step, state)
    for h in range(FOX_LAG):
        hp = h + G - FOX_LAG
        consume(kb_prev, hp, pending[h], ms[hp], 0)
    o_ref[0] = jnp.concatenate([acc[h, 0:dh] / acc[h, dh:dh + 1] for h in range(G)], axis=0).T.astype(BF16)


def _fox_attn(qt, k, kc, vt, crow):
    B, S, D = k.shape
    G = FOX_GROUP
    W = G * (LANES // 2)
    return pl.pallas_call(
        _fox_kernel,
        grid=(B, D // W, S // TQ),
        in_specs=[
            pl.BlockSpec((1, W, TQ), lambda b, g, i: (b, g, i)),
            pl.BlockSpec((1, S, W), lambda b, g, i: (b, 0, g)),
            pl.BlockSpec((1, S, W), lambda b, g, i: (b, 0, g)),
            pl.BlockSpec((1, W, S), lambda b, g, i: (b, g, 0)),
            pl.BlockSpec((1, FOX_HEADS, TQ), lambda b, g, i: (b, 0, i)),
        ],
        out_specs=pl.BlockSpec((1, TQ, W), lambda b, g, i: (b, i, g)),
        out_shape=jax.ShapeDtypeStruct((B, S, D), BF16),
        scratch_shapes=[pltpu.VMEM((G, TK, TQ), F32), pltpu.VMEM((G, LANES // 2 + SUBLANES, TQ), F32)],
        compiler_params=pltpu.CompilerParams(
            dimension_semantics=("arbitrary", "arbitrary", "arbitrary"), vmem_limit_bytes=VMEM_LIMIT),
        name="fox_attn",
    )(qt, k, kc, vt, crow)


def kernel(x, mem, norm_mix_g, norm_mem_g, norm_memsrc_g, norm_ff_g, mem_wq, mem_wk, mem_wv, mem_wo, ff_w1, ff_w2,
           conv_pw1_w, conv_pw1_b, conv_dw_w, conv_dw_b, conv_ln_g, conv_ln_b, conv_pw2_w, conv_pw2_b,
           kv_norm_g, kvf_w, fgate_b, fox_wq, fox_wo, final_norm_g):
    D = x.shape[2]
    depth = norm_mix_g.shape[0]
    n_conv = conv_pw1_w.shape[0]
    bf = lambda w: w.astype(BF16)
    layer_w = lambda w, l: bf(w[l:l + 1])
    mem_wk_b, mem_wv_b = bf(mem_wk), bf(mem_wv)
    pw1_b, pw2_b = bf(conv_pw1_w), bf(conv_pw2_w)
    fox_wqt_b, fox_wo_b = bf(fox_wq).transpose(0, 2, 1), bf(fox_wo)
    kvf_b = bf(kvf_w)
    wk_b = kvf_b[:, :D]
    wvt_b = kvf_b[:, D:2 * D].T
    wf_b = jnp.pad(kvf_b[:, 2 * D:], ((0, 0), (0, LANES - FOX_HEADS)))
    fb = jnp.pad(fgate_b, (0, LANES - FOX_HEADS)).reshape(1, LANES)

    assert depth == 2 and n_conv == 1 and fox_wq.shape[0] == 1

    mk, mv = _mem_kv(mem, norm_memsrc_g, mem_wk_b, mem_wv_b)
    def tail_args(l):
        return dict(layer=l, gmem=norm_mem_g, wq=layer_w(mem_wq, l), k=mk, v=mv, wo=layer_w(mem_wo, l),
                    gff=norm_ff_g, w1=layer_w(ff_w1, l), w2=layer_w(ff_w2, l))

    conv_params = (norm_mix_g[0], pw1_b[0], conv_pw1_b[0], conv_dw_w[0], conv_dw_b[0],
                   conv_ln_g[0], conv_ln_b[0], pw2_b[0], conv_pw2_b[0])
    x = _mix_tail(x, conv_params, **tail_args(0))
    k, kc, vt, qt, crow = _kvfq(x, kv_norm_g, norm_mix_g[1], wk_b, wvt_b, wf_b, fb, fox_wqt_b, 0)
    o = _fox_attn(qt, k, kc, vt, crow)
    return _tail(x, attn=(o, fox_wo_b), gfin=final_norm_g, **tail_args(1))
```

```python
import functools
import math

import jax
import jax.numpy as jnp
import numpy as np
from jax import lax
from jax.experimental import pallas as pl
from jax.experimental.pallas import tpu as pltpu

F32 = jnp.float32
BF16 = jnp.bfloat16

CONV_WIDTH = 31
FOX_HEADS = 16
MEM_HEADS = 4
RMS_EPS = 1e-6
LN_EPS = 1e-5
LOG2E = 1.4426950408889634

LANES = 128
SUBLANES = 8
HALO = 32
TM = 512
TQ = 512
TK = 512
VMEM_LIMIT = 56 * 1024 * 1024


def _rms(x, g):
    return x * lax.rsqrt(jnp.mean(x * x, axis=-1, keepdims=True) + RMS_EPS) * g


def _mm(a, w):
    return jnp.dot(a.astype(BF16), w, preferred_element_type=F32)


def _mm_nt(a, b):
    return lax.dot_general(a, b, (((1,), (1,)), ((), ())), preferred_element_type=F32)


def _resident(shape, index_map):
    return pl.BlockSpec(shape, index_map, pipeline_mode=pl.Buffered(1))


def _mem_kv_kernel(mem_ref, g_ref, wk_ref, wv_ref, k_ref, v_ref):
    mn = _rms(mem_ref[0], g_ref[0]).astype(BF16)
    k_ref[0, 0] = jnp.dot(mn, wk_ref[0], preferred_element_type=F32).astype(BF16)
    v_ref[0, 0] = jnp.dot(mn, wv_ref[0], preferred_element_type=F32).astype(BF16)


def _mem_kv(mem, g, wk, wv):
    B, M, D = mem.shape
    L = wk.shape[0]
    out = jax.ShapeDtypeStruct((L, B, M, D), BF16)
    return pl.pallas_call(
        _mem_kv_kernel,
        grid=(L, B),
        in_specs=[
            pl.BlockSpec((1, M, D), lambda l, b: (b, 0, 0)),
            pl.BlockSpec((1, 1, D), lambda l, b: (l, 0, 0)),
            pl.BlockSpec((1, D, D), lambda l, b: (l, 0, 0)),
            pl.BlockSpec((1, D, D), lambda l, b: (l, 0, 0)),
        ],
        out_specs=[
            pl.BlockSpec((1, 1, M, D), lambda l, b: (l, b, 0, 0)),
            pl.BlockSpec((1, 1, M, D), lambda l, b: (l, b, 0, 0)),
        ],
        out_shape=[out, out],
        compiler_params=pltpu.CompilerParams(
            dimension_semantics=("arbitrary", "arbitrary"), vmem_limit_bytes=VMEM_LIMIT),
        name="mem_kv",
    )(mem, g.reshape(L, 1, D), wk, wv)


CONV_ROWS = 128
CONV_COLS = LANES


def _conv_mixer_stages(x, g_ref, w1_ref, b1_ref, wdw_ref, bdw_ref, lng_ref, lnb_ref, w2_ref, b2_ref, ubuf, cbuf):
    tm, d = x.shape
    u = _mm(_rms(x, g_ref[...]), w1_ref[...]) + b1_ref[...]
    ubuf[HALO:HALO + tm, :] = u[:, :d] * jax.nn.sigmoid(u[:, d:])
    yield None

    base = HALO - (CONV_WIDTH - 1)
    for r0 in range(0, tm, CONV_ROWS):
        for c0 in range(0, d, CONV_COLS):
            cols = slice(c0, c0 + CONV_COLS)
            win = ubuf[r0:r0 + CONV_ROWS + HALO, cols]
            acc = jnp.broadcast_to(bdw_ref[:, cols], (CONV_ROWS, CONV_COLS))
            for r in range(SUBLANES):
                taps = [o for o in range(base, base + CONV_WIDTH) if o % SUBLANES == r]
                sh = win if r == 0 else pltpu.roll(win, CONV_ROWS + HALO - r, axis=0)
                for o in taps:
                    acc = acc + wdw_ref[o - base:o - base + 1, cols] * sh[o - r:o - r + CONV_ROWS]
            cbuf[r0:r0 + CONV_ROWS, cols] = acc
            yield None
    ubuf[0:HALO, :] = ubuf[tm:tm + HALO, :]

    c = cbuf[...]
    mu = jnp.mean(c, axis=-1, keepdims=True)
    cc = c - mu
    var = jnp.mean(cc * cc, axis=-1, keepdims=True)
    y = cc * lax.rsqrt(var + LN_EPS) * lng_ref[...] + lnb_ref[...]
    y = y * jax.nn.sigmoid(y)
    yield x + _mm(y, w2_ref[...]) + b2_ref[...]


FF_CHUNK = 1024


def _mem_attn_mlp_stages(x, gmem_ref, wq_ref, k_ref, v_ref, wo_ref, gff_ref, w1_ref, w2_ref):
    dh = x.shape[1] // MEM_HEADS
    q = (_mm(_rms(x, gmem_ref[0]), wq_ref[0]) * (1.0 / math.sqrt(dh))).astype(BF16)
    yield None
    heads = []
    for hd in range(MEM_HEADS):
        sl = slice(hd * dh, (hd + 1) * dh)
        s = _mm_nt(q[:, sl], k_ref[0, 0, :, sl])
        p = jnp.exp(s - jnp.max(s, axis=-1, keepdims=True))
        l = jnp.sum(p, axis=-1, keepdims=True)
        heads.append(_mm(p, v_ref[0, 0, :, sl]) / l)
    yield None
    x = x + _mm(jnp.concatenate(heads, axis=-1), wo_ref[0])
    yield None

    hb = _rms(x, gff_ref[0]).astype(BF16)
    acc = jnp.zeros_like(x)
    for c in range(w1_ref.shape[2] // FF_CHUNK):
        sl = slice(c * FF_CHUNK, (c + 1) * FF_CHUNK)
        a = jnp.maximum(jnp.dot(hb, w1_ref[0, :, sl], preferred_element_type=F32), 0.0)
        yield None
        acc = acc + _mm(a * a, w2_ref[0, sl, :])
        yield None
    yield x + acc


def _run(stages):
    for result in stages:
        pass
    return result


def _interleave(a_stages, b_stages, b_per_a):
    ra = rb = None
    a_live = b_live = True
    while a_live or b_live:
        if a_live:
            try:
                ra = next(a_stages)
            except StopIteration:
                a_live = False
        n_b = 0
        while b_live and (n_b < b_per_a or not a_live):
            try:
                rb = next(b_stages)
                n_b += 1
            except StopIteration:
                b_live = False
    return ra, rb


def _tail_kernel(*refs, has_attn, final):
    it = iter(refs)
    x_ref = next(it)
    if has_attn:
        oin_ref, fwo_ref = next(it), next(it)
    tail_refs = [next(it) for _ in range(8)]
    if final:
        gfin_ref = next(it)
    out_ref = next(it)

    x = x_ref[0]
    if has_attn:
        x = x + jnp.dot(oin_ref[0], fwo_ref[0], preferred_element_type=F32)
    x = _run(_mem_attn_mlp_stages(x, *tail_refs))
    if final:
        x = _rms(x, gfin_ref[...])
    out_ref[0] = x


MIXER_STAGES_PER_TAIL_STAGE = 3


def _mix_tail_kernel(x_ref, *refs, tiles_per_seq):
    conv_refs, tail_refs = refs[:9], refs[9:17]
    out_ref, ubuf, cbuf, x1buf = refs[17:]
    t = pl.program_id(0)

    @pl.when(t == 0)
    def _():
        x1buf[...] = jnp.zeros_like(x1buf)

    @pl.when(t % tiles_per_seq == 0)
    def _():
        ubuf[0:HALO, :] = jnp.zeros((HALO, ubuf.shape[1]), F32)

    out, x1_next = _interleave(_mem_attn_mlp_stages(x1buf[...], *tail_refs),
                               _conv_mixer_stages(x_ref[0], *conv_refs, ubuf, cbuf), MIXER_STAGES_PER_TAIL_STAGE)
    out_ref[0] = out
    x1buf[...] = x1_next


def _mix_tail(x, conv_params, layer, gmem, wq, k, v, wo, gff, w1, w2):
    g, cw1, cb1, wdw, bdw, lng, lnb, cw2, cb2 = conv_params
    B, S, D = x.shape
    L = gmem.shape[0]
    M = k.shape[2]
    F = w1.shape[2]
    n_seq = S // TM
    n_tiles = B * n_seq
    row = lambda v: v.reshape(1, -1)
    const = lambda t: (0, 0)
    lay = lambda t: (layer, 0, 0)
    first = lambda t: (0, 0, 0)

    def mixer_tile(t):
        t = jnp.minimum(t, n_tiles - 1)
        return t // n_seq, t % n_seq, 0

    def tail_tile(t):
        t = jnp.maximum(t - 1, 0)
        return t // n_seq, t % n_seq, 0

    mem_block = pl.BlockSpec((1, 1, M, D), lambda t: (layer, jnp.maximum(t - 1, 0) // n_seq, 0, 0))
    args = (x, row(g), cw1, row(cb1), wdw, row(bdw), row(lng), row(lnb), cw2, row(cb2),
            gmem.reshape(L, 1, D), wq, k, v, wo, gff.reshape(L, 1, D), w1, w2)
    weights = (cw1, cw2, wq, wo, w1, w2)
    return pl.pallas_call(
        functools.partial(_mix_tail_kernel, tiles_per_seq=n_seq),
        grid=(n_tiles + 1,),
        in_specs=[
            pl.BlockSpec((1, TM, D), mixer_tile),
            _resident((1, D), const),
            _resident((D, 2 * D), const),
            _resident((1, 2 * D), const),
            _resident((CONV_WIDTH, D), const),
            _resident((1, D), const),
            _resident((1, D), const),
            _resident((1, D), const),
            _resident((D, D), const),
            _resident((1, D), const),
            _resident((1, 1, D), lay),
            _resident((1, D, D), first),
            mem_block,
            mem_block,
            _resident((1, D, D), first),
            _resident((1, 1, D), lay),
            _resident((1, D, F), first),
            _resident((1, F, D), first),
        ],
        out_specs=pl.BlockSpec((1, TM, D), tail_tile),
        out_shape=jax.ShapeDtypeStruct((B, S, D), F32),
        scratch_shapes=[pltpu.VMEM((TM + HALO, D), F32), pltpu.VMEM((TM, D), F32), pltpu.VMEM((TM, D), F32)],
        compiler_params=pltpu.CompilerParams(
            dimension_semantics=("arbitrary",), vmem_limit_bytes=VMEM_LIMIT,
            allow_input_fusion=[any(a is w for w in weights) for a in args]),
        name="mix_tail",
    )(*args)


def _tail(x, layer, gmem, wq, k, v, wo, gff, w1, w2, attn=None, gfin=None):
    B, S, D = x.shape
    L = gmem.shape[0]
    M = k.shape[2]
    F = w1.shape[2]
    lay = lambda b, s: (layer, 0, 0)
    first = lambda b, s: (0, 0, 0)
    tile = pl.BlockSpec((1, TM, D), lambda b, s: (b, s, 0))
    args, specs = [x], [tile]
    if attn is not None:
        o_in, fwo = attn
        args += [o_in, fwo]
        specs += [pl.BlockSpec((1, TM, D), lambda b, s: (b, s, 0)), _resident((1, D, D), first)]
    args += [gmem.reshape(L, 1, D), wq, k, v, wo, gff.reshape(L, 1, D), w1, w2]
    specs += [
        _resident((1, 1, D), lay),
        _resident((1, D, D), first),
        pl.BlockSpec((1, 1, M, D), lambda b, s: (layer, b, 0, 0)),
        pl.BlockSpec((1, 1, M, D), lambda b, s: (layer, b, 0, 0)),
        _resident((1, D, D), first),
        _resident((1, 1, D), lay),
        _resident((1, D, F), first),
        _resident((1, F, D), first),
    ]
    if gfin is not None:
        args.append(gfin.reshape(1, D))
        specs.append(_resident((1, D), lambda b, s: (0, 0)))
    weights = (wq, wo, w1, w2) + (() if attn is None else (attn[1],))
    return pl.pallas_call(
        functools.partial(_tail_kernel, has_attn=attn is not None, final=gfin is not None),
        grid=(B, S // TM),
        in_specs=specs,
        out_specs=tile,
        out_shape=jax.ShapeDtypeStruct((B, S, D), F32),
        compiler_params=pltpu.CompilerParams(
            dimension_semantics=("arbitrary", "arbitrary"), vmem_limit_bytes=VMEM_LIMIT,
            allow_input_fusion=[any(a is w for w in weights) for a in args]),
        name=f"tail{layer}",
    )(*args)


VT_ROWS = 80
C_PIECES = 3


def _bf16_pieces(x):
    pieces, rest = [], x
    for _ in range(C_PIECES):
        piece = rest.astype(BF16)
        pieces.append(piece)
        rest = rest - piece.astype(F32)
    return jnp.concatenate(pieces, axis=1)


def _kvfq_kernel(x_ref, gkv_ref, gq_ref, wk_ref, wvt_ref, ones_ref, wf_ref, fb_ref, place_ref, wq_ref,
                 k_ref, kc_ref, vt_ref, q_ref, crow_ref, carry):
    tm = x_ref.shape[1]
    dh = x_ref.shape[2] // FOX_HEADS

    @pl.when(pl.program_id(1) == 0)
    def _():
        carry[...] = jnp.zeros_like(carry)

    x = x_ref[0]
    hk = _rms(x, gkv_ref[...]).astype(BF16)
    k_ref[0] = jnp.dot(hk, wk_ref[...], preferred_element_type=F32).astype(BF16)
    vt_ref[0] = (_mm_nt(wvt_ref[...], hk) + jnp.tile(ones_ref[...], (1, tm // LANES))).astype(BF16)

    f = jnp.dot(hk, wf_ref[...], preferred_element_type=F32) + fb_ref[...]
    log_f = jnp.minimum(f, 0.0) - jnp.log(1.0 + jnp.exp(-jnp.abs(f)))
    tri = (lax.broadcasted_iota(jnp.int32, (tm, tm), 1)
           <= lax.broadcasted_iota(jnp.int32, (tm, tm), 0)).astype(BF16)
    parts = jnp.dot(tri, _bf16_pieces(log_f), preferred_element_type=F32)
    c = sum(parts[:, i * LANES:(i + 1) * LANES] for i in range(C_PIECES)) + carry[0:1, :]
    carry[0:1, :] = c[tm - 1:tm, :]
    c2 = c * LOG2E
    crow_ref[0] = c2.T[0:FOX_HEADS, :]

    kc_ref[0] = jnp.dot(_bf16_pieces(c2), place_ref[...], preferred_element_type=F32).astype(BF16)

    hq = _rms(x, gq_ref[...]).astype(BF16)
    q_ref[0] = (_mm_nt(wq_ref[0], hq) * (LOG2E / math.sqrt(dh))).astype(BF16)


def _gate_placement(n_heads):
    place = np.zeros((C_PIECES * LANES, n_heads // 2 * LANES), np.float32)
    for h in range(n_heads):
        for i in range(C_PIECES):
            place[i * LANES + h, (h // 2) * LANES + (h % 2) * C_PIECES + i] = 1.0
    return jnp.asarray(place, BF16)


def _kvfq(x, gkv, gq, wk, wvt, vt_ones, wf, fb, wq, layer):
    place = _gate_placement(FOX_HEADS)
    B, S, D = x.shape
    R = wvt.shape[0]
    const = lambda b, s: (0, 0)
    tile = pl.BlockSpec((1, TM, D), lambda b, s: (b, s, 0))
    return pl.pallas_call(
        _kvfq_kernel,
        grid=(B, S // TM),
        in_specs=[
            tile,
            _resident((1, D), const),
            _resident((1, D), const),
            _resident((D, D), const),
            _resident((R, D), const),
            _resident((R, LANES), const),
            _resident((D, LANES), const),
            _resident((1, LANES), const),
            _resident(place.shape, const),
            _resident((1, D, D), lambda b, s: (layer, 0, 0)),
        ],
        out_specs=[
            tile,
            tile,
            pl.BlockSpec((1, R, TM), lambda b, s: (b, 0, s)),
            pl.BlockSpec((1, D, TM), lambda b, s: (b, 0, s)),
            pl.BlockSpec((1, FOX_HEADS, TM), lambda b, s: (b, 0, s)),
        ],
        out_shape=[
            jax.ShapeDtypeStruct((B, S, D), BF16),
            jax.ShapeDtypeStruct((B, S, D), BF16),
            jax.ShapeDtypeStruct((B, R, S), BF16),
            jax.ShapeDtypeStruct((B, D, S), BF16),
            jax.ShapeDtypeStruct((B, FOX_HEADS, S), F32),
        ],
        scratch_shapes=[pltpu.VMEM((8, LANES), F32)],
        compiler_params=pltpu.CompilerParams(
            dimension_semantics=("arbitrary", "arbitrary"), vmem_limit_bytes=VMEM_LIMIT),
        name="kvfq",
    )(x, gkv.reshape(1, D), gq.reshape(1, D), wk, wvt, vt_ones, wf, fb, place, wq)


FOX_GROUP = 8
FOX_LAG = 2


def _fox_kernel(q_ref, k_ref, kc_ref, vt_ref, crow_ref, o_ref, tbuf, acc):
    tq = q_ref.shape[2]
    dh = LANES // 2
    grp = pl.program_id(1)
    qi = pl.program_id(2)
    G = FOX_GROUP

    tk = tbuf.shape[1]
    n_sub = tq // tk
    row = lax.broadcasted_iota(jnp.int32, (LANES, tq), 0)
    key_minus_query = (lax.broadcasted_iota(jnp.int32, (tk, tq), 0) - lax.broadcasted_iota(jnp.int32, (tk, tq), 1))
    qx, cqs = [], []
    for h in range(G):
        pr, hh = divmod(h, 2)
        qp = q_ref[0, pr * LANES:(pr + 1) * LANES, :]
        gate = (row >= hh * C_PIECES) & (row < (hh + 1) * C_PIECES)
        qx.append(jnp.concatenate([jnp.where(row // dh == hh, qp, jnp.zeros_like(qp)),
                                   jnp.where(gate, -1.0, 0.0).astype(BF16)], axis=0))
        cqs.append(crow_ref[0, pl.ds(G * grp + h, 1), :])
        acc[h] = jnp.zeros(acc.shape[1:], F32)

    def scores(kb, h, diag_offset):
        pr = h // 2
        col0 = diag_offset or 0
        k0 = pl.multiple_of(kb * tk, tk)
        kx = jnp.concatenate([k_ref[0, pl.ds(k0, tk), pr * LANES:(pr + 1) * LANES],
                              kc_ref[0, pl.ds(k0, tk), pr * LANES:(pr + 1) * LANES]], axis=1)
        t = jnp.dot(kx, qx[h][:, col0:], preferred_element_type=F32)
        if diag_offset is not None:
            t = jnp.where(key_minus_query[:, col0:] <= -diag_offset, t, -jnp.inf)
        tbuf[h, :, col0:] = t
        return jnp.max(t, axis=0, keepdims=True)

    def consume(kb, h, mb, m, col0):
        k0 = pl.multiple_of(kb * tk, tk)
        cq, m_old = cqs[h][:, col0:], m[:, col0:]
        m_new = jnp.maximum(m_old, mb + cq)
        p = jnp.exp2(tbuf[h, :, col0:] + (cq - m_new)).astype(BF16)
        v = vt_ref[0, h * VT_ROWS:(h + 1) * VT_ROWS, pl.ds(k0, tk)]
        acc[h, :, col0:] = jnp.exp2(m_old - m_new) * acc[h, :, col0:] + jnp.dot(v, p, preferred_element_type=F32)
        return m_new if col0 == 0 else jnp.concatenate([m[:, :col0], m_new], axis=1)

    def sweep(kb, state, diag_offset=None, pending_col0=0):
        kb_prev, pending, ms = state
        ms, mbs = list(ms), []
        for h in range(G):
            mbs.append(scores(kb, h, diag_offset))
            if h >= FOX_LAG:
                ms[h - FOX_LAG] = consume(kb, h - FOX_LAG, mbs[h - FOX_LAG], ms[h - FOX_LAG], diag_offset or 0)
            elif pending is not None:
                hp = h + G - FOX_LAG
                ms[hp] = consume(kb_prev, hp, pending[h], ms[hp], pending_col0)
        return kb, tuple(mbs[G - FOX_LAG:]), tuple(ms)

    def step(j, state):
        for i in range(n_sub):
            state = sweep(n_sub * j + i, state)
        return state

    state = (None, None, (jnp.full((1, tq), -jnp.inf, F32),) * G)
    pending_col0 = 0
    for i in reversed(range(n_sub)):
        state = sweep(n_sub * qi + i, state, diag_offset=i * tk, pending_col0=pending_col0)
        pending_col0 = i * tk
    kb_prev, pending, ms = lax.fori_loop(0, qi, step, state)
    for h in range(FOX_LAG):
        hp = h + G - FOX_LAG
        consume(kb_prev, hp, pending[h], ms[hp], 0)
    o_ref[0] = jnp.concatenate([acc[h, 0:dh] / acc[h, dh:dh + 1] for h in range(G)], axis=0).T.astype(BF16)


def _fox_attn(qt, k, kc, vt, crow):
    B, S, D = k.shape
    G = FOX_GROUP
    W = G * (LANES // 2)
    return pl.pallas_call(
        _fox_kernel,
        grid=(B, D // W, S // TQ),
        in_specs=[
            pl.BlockSpec((1, W, TQ), lambda b, g, i: (b, g, i)),
            pl.BlockSpec((1, S, W), lambda b, g, i: (b, 0, g)),
            pl.BlockSpec((1, S, W), lambda b, g, i: (b, 0, g)),
            pl.BlockSpec((1, G * VT_ROWS, S), lambda b, g, i: (b, g, 0)),
            pl.BlockSpec((1, FOX_HEADS, TQ), lambda b, g, i: (b, 0, i)),
        ],
        out_specs=pl.BlockSpec((1, TQ, W), lambda b, g, i: (b, i, g)),
        out_shape=jax.ShapeDtypeStruct((B, S, D), BF16),
        scratch_shapes=[pltpu.VMEM((G, TK, TQ), F32), pltpu.VMEM((G, VT_ROWS, TQ), F32)],
        compiler_params=pltpu.CompilerParams(
            dimension_semantics=("arbitrary", "arbitrary", "arbitrary"), vmem_limit_bytes=VMEM_LIMIT),
        name="fox_attn",
    )(qt, k, kc, vt, crow)


def kernel(x, mem, norm_mix_g, norm_mem_g, norm_memsrc_g, norm_ff_g, mem_wq, mem_wk, mem_wv, mem_wo, ff_w1, ff_w2,
           conv_pw1_w, conv_pw1_b, conv_dw_w, conv_dw_b, conv_ln_g, conv_ln_b, conv_pw2_w, conv_pw2_b,
           kv_norm_g, kvf_w, fgate_b, fox_wq, fox_wo, final_norm_g):
    D = x.shape[2]
    depth = norm_mix_g.shape[0]
    n_conv = conv_pw1_w.shape[0]
    bf = lambda w: w.astype(BF16)
    layer_w = lambda w, l: bf(w[l:l + 1])
    mem_wk_b, mem_wv_b = bf(mem_wk), bf(mem_wv)
    pw1_b, pw2_b = bf(conv_pw1_w), bf(conv_pw2_w)
    fox_wqt_b, fox_wo_b = bf(fox_wq).transpose(0, 2, 1), bf(fox_wo)
    kvf_b = bf(kvf_w)
    wk_b = kvf_b[:, :D]
    dh = D // FOX_HEADS
    wvt_b = jnp.pad(kvf_b[:, D:2 * D].T.reshape(FOX_HEADS, dh, D),
                    ((0, 0), (0, VT_ROWS - dh), (0, 0))).reshape(FOX_HEADS * VT_ROWS, D)
    vt_ones = jnp.tile((jnp.arange(VT_ROWS) >= dh).astype(F32)[:, None], (FOX_HEADS, LANES))
    wf_b = jnp.pad(kvf_b[:, 2 * D:], ((0, 0), (0, LANES - FOX_HEADS)))
    fb = jnp.pad(fgate_b, (0, LANES - FOX_HEADS)).reshape(1, LANES)

    assert depth == 2 and n_conv == 1 and fox_wq.shape[0] == 1

    mk, mv = _mem_kv(mem, norm_memsrc_g, mem_wk_b, mem_wv_b)
    def tail_args(l):
        return dict(layer=l, gmem=norm_mem_g, wq=layer_w(mem_wq, l), k=mk, v=mv, wo=layer_w(mem_wo, l),
                    gff=norm_ff_g, w1=layer_w(ff_w1, l), w2=layer_w(ff_w2, l))

    conv_params = (norm_mix_g[0], pw1_b[0], conv_pw1_b[0], conv_dw_w[0], conv_dw_b[0],
                   conv_ln_g[0], conv_ln_b[0], pw2_b[0], conv_pw2_b[0])
    x = _mix_tail(x, conv_params, **tail_args(0))
    k, kc, vt, qt, crow = _kvfq(x, kv_norm_g, norm_mix_g[1], wk_b, wvt_b, vt_ones, wf_b, fb, fox_wqt_b, 0)
    o = _fox_attn(qt, k, kc, vt, crow)
    return _tail(x, attn=(o, fox_wo_b), gfin=final_norm_g, **tail_args(1))
```
